```python
import jax, jax.numpy as jnp
from jax import lax
import numpy as np

D_MODEL = 4096
BATCH = 2
SEQ = 4096
DEPTH = 4

CHUNK = 64
LEFT_CHUNKS = 8
BAND = (LEFT_CHUNKS + 1) * CHUNK

MIX_WIDTH = D_MODEL
A_WIDTH = MIX_WIDTH // 2
A_HEAD_DIM = 128
A_HEADS = A_WIDTH // A_HEAD_DIM
REL_CLIP = 128
N_REL = 2 * REL_CLIP + 1
B_WIDTH = MIX_WIDTH - A_WIDTH
B_HEADS = 4
B_HEAD_V = B_WIDTH // B_HEADS
B_KEY_WIDTH = B_WIDTH // 2
B_HEAD_K = B_KEY_WIDTH // B_HEADS
GATE_RANK = 16
GATE_TEMP = 16.0

FFN_HIDDEN = ((8 * D_MODEL + 3 * 256 - 1) // (3 * 256)) * 256
EPS = 1e-6

_COLS = (A_WIDTH, A_WIDTH, A_WIDTH, B_KEY_WIDTH, B_KEY_WIDTH, B_WIDTH, B_WIDTH, GATE_RANK)
IN_COLS = sum(_COLS)
SPLITS = tuple(int(v) for v in np.cumsum(_COLS)[:-1])

kernel_name = "hybrid_chunkattn_gla_sandwich"


def rmsnorm(x, w):
    xf = x.astype(jnp.float32)
    y = xf * lax.rsqrt(jnp.mean(xf * xf, axis=-1, keepdims=True) + EPS) * w.astype(jnp.float32)
    return y.astype(x.dtype)


def chunk_band_attention(q, k, v, rel_bias):
    b, s, h, dh = q.shape
    nc = s // CHUNK
    qc = q.reshape(b, nc, CHUNK, h, dh)

    def band(t):
        tc = t.reshape(b, nc, CHUNK, h, dh)
        tp = jnp.pad(tc, ((0, 0), (LEFT_CHUNKS, 0), (0, 0), (0, 0), (0, 0)))
        return jnp.concatenate([tp[:, p:p + nc] for p in range(LEFT_CHUNKS + 1)], axis=2)

    kb, vb = band(k), band(v)
    scores = jnp.einsum('bclhd,bcnhd->bhcln', qc, kb).astype(jnp.float32) * (dh ** -0.5)
    l_idx = jnp.arange(CHUNK)[:, None]
    n_idx = jnp.arange(BAND)[None, :]
    dist = jnp.clip(LEFT_CHUNKS * CHUNK + l_idx - n_idx, -REL_CLIP, REL_CLIP) + REL_CLIP
    bias = rel_bias[:, dist].astype(jnp.float32)
    scores = scores + bias[:, None]
    valid = (n_idx // CHUNK) >= (LEFT_CHUNKS - jnp.arange(nc)[:, None])
    scores = jnp.where(valid[:, None, :], scores, -1e30)
    probs = jax.nn.softmax(scores, axis=-1).astype(v.dtype)
    out = jnp.einsum('bhcln,bcnhv->bclhv', probs, vb)
    return out.reshape(b, s, h * dh)


def gla_chunked(q, k, v, log_a):
    b, s, h, dk = q.shape
    dv = v.shape[-1]
    nc = s // CHUNK

    def r(t):
        return t.reshape(b, nc, CHUNK, h, t.shape[-1])

    q, k, v, log_a = r(q * (dk ** -0.5)), r(k), r(v), r(log_a)
    cum = jnp.cumsum(log_a, axis=2)
    last = cum[:, :, -1:]
    q_dec = q * jnp.exp(cum)
    k_inv = k * jnp.exp(-cum)
    k_tail = k * jnp.exp(last - cum)
    causal = jnp.tril(jnp.ones((CHUNK, CHUNK), dtype=bool))
    att = jnp.where(causal, jnp.einsum('bclhd,bcmhd->bchlm', q_dec, k_inv), 0.0)
    o_intra = jnp.einsum('bchlm,bcmhv->bclhv', att, v)
    upd = jnp.einsum('bclhd,bclhv->cbhdv', k_tail, v)
    decay = jnp.exp(last[:, :, 0]).transpose(1, 0, 2, 3)

    def step(state, inp):
        d, u = inp
        return d[..., None] * state + u, state

    s0 = jnp.zeros((b, h, dk, dv), q.dtype)
    _, states = lax.scan(step, s0, (decay, upd))
    o_inter = jnp.einsum('bclhd,cbhdv->bclhv', q_dec, states)
    return (o_intra + o_inter).reshape(b, s, h, dv)


def hybrid_mixer(u, w_in, rel_bias, w_fg_up, b_fg, gla_norm_w, w_out):
    b, s, _ = u.shape
    f32 = jnp.float32
    proj = u @ w_in
    qa, ka, va, qb, kb, vb, gb, fb = jnp.split(proj, SPLITS, axis=-1)
    shp_a = (b, s, A_HEADS, A_HEAD_DIM)
    ya = chunk_band_attention(qa.reshape(shp_a), ka.reshape(shp_a), va.reshape(shp_a), rel_bias)
    log_a = jax.nn.log_sigmoid((fb @ w_fg_up + b_fg).astype(f32)) / GATE_TEMP
    shp_k = (b, s, B_HEADS, B_HEAD_K)
    ob = gla_chunked(qb.reshape(shp_k).astype(f32), kb.reshape(shp_k).astype(f32),
                     vb.reshape(b, s, B_HEADS, B_HEAD_V).astype(f32), log_a.reshape(shp_k))
    ob = ob * lax.rsqrt(jnp.mean(ob * ob, axis=-1, keepdims=True) + EPS) * gla_norm_w.astype(f32)
    yb = (ob.reshape(b, s, B_WIDTH) * jax.nn.silu(gb.astype(f32))).astype(u.dtype)
    return jnp.concatenate([ya, yb], axis=-1) @ w_out


def swiglu(u, w_gate, w_up, w_down):
    return (jax.nn.silu(u @ w_gate) * (u @ w_up)) @ w_down


def setup_inputs(seed: int = 0) -> dict:
    key = jax.random.key(seed)
    ks = jax.random.split(key, 16)
    f32 = jnp.float32

    def nrm(k, shape, scale):
        return jax.random.normal(k, shape, f32) * scale

    def gain(k, shape):
        return 1.0 + 0.05 * jax.random.normal(k, shape, f32)

    return {
        "x": jax.random.normal(ks[0], (BATCH, SEQ, D_MODEL), f32),
        "norm_mix_pre": gain(ks[1], (DEPTH, D_MODEL)),
        "norm_mix_post": gain(ks[2], (DEPTH, D_MODEL)),
        "norm_ffn_pre": gain(ks[3], (DEPTH, D_MODEL)),
        "norm_ffn_post": gain(ks[4], (DEPTH, D_MODEL)),
        "w_in": nrm(ks[5], (DEPTH, D_MODEL, IN_COLS), D_MODEL ** -0.5),
        "rel_bias": nrm(ks[6], (DEPTH, A_HEADS, N_REL), 0.5),
        "w_fgate_up": nrm(ks[7], (DEPTH, GATE_RANK, B_KEY_WIDTH), GATE_RANK ** -0.5),
        "b_fgate": nrm(ks[8], (DEPTH, B_KEY_WIDTH), 0.1),
        "gla_norm": gain(ks[9], (DEPTH, B_HEAD_V)),
        "w_out": nrm(ks[10], (DEPTH, MIX_WIDTH, D_MODEL), MIX_WIDTH ** -0.5),
        "w_ffn_gate": nrm(ks[11], (DEPTH, D_MODEL, FFN_HIDDEN), D_MODEL ** -0.5),
        "w_ffn_up": nrm(ks[12], (DEPTH, D_MODEL, FFN_HIDDEN), D_MODEL ** -0.5),
        "w_ffn_down": nrm(ks[13], (DEPTH, FFN_HIDDEN, D_MODEL), FFN_HIDDEN ** -0.5),
    }


def reference(x, norm_mix_pre, norm_mix_post, norm_ffn_pre, norm_ffn_post, w_in, rel_bias,
              w_fgate_up, b_fgate, gla_norm, w_out, w_ffn_gate, w_ffn_up, w_ffn_down):
    h = x
    for i in range(DEPTH):
        u = rmsnorm(h, norm_mix_pre[i])
        m = hybrid_mixer(u, w_in[i], rel_bias[i], w_fgate_up[i], b_fgate[i], gla_norm[i], w_out[i])
        h = h + rmsnorm(m, norm_mix_post[i])
        u = rmsnorm(h, norm_ffn_pre[i])
        f = swiglu(u, w_ffn_gate[i], w_ffn_up[i], w_ffn_down[i])
        h = h + rmsnorm(f, norm_ffn_post[i])
    return h
```

```python
import functools

import jax
import jax.numpy as jnp
from jax import lax
from jax.experimental import pallas as pl
from jax.experimental.pallas import tpu as pltpu

D_MODEL = 4096
CHUNK = 64
LEFT_CHUNKS = 8
A_WIDTH = 2048
A_HEAD_DIM = 128
A_HEADS = A_WIDTH // A_HEAD_DIM
REL_CLIP = 128
B_WIDTH = 2048
B_HEADS = 4
B_HEAD_V = B_WIDTH // B_HEADS
B_KEY_WIDTH = B_WIDTH // 2
B_HEAD_K = B_KEY_WIDTH // B_HEADS
GATE_RANK = 16
GATE_TEMP = 16.0
EPS = 1e-6
NEG_INF = -1e30

LANES = 128
VMEM_LIMIT = 56 * 1024 * 1024

COL_QA = 0
COL_KA = A_WIDTH
COL_VA = 2 * A_WIDTH
COL_QB = 3 * A_WIDTH
COL_KB = COL_QB + B_KEY_WIDTH
COL_VB = COL_KB + B_KEY_WIDTH
COL_GB = COL_VB + B_WIDTH
MAIN_COLS = COL_GB + B_WIDTH

ATT_TQ = 256
ATT_NKB = 3
ATT_TK = ATT_NKB * ATT_TQ
BIAS_W = 1024


def _params(semantics):
    return pltpu.CompilerParams(dimension_semantics=semantics,
                                vmem_limit_bytes=VMEM_LIMIT)


def _rms(x):
    return x * lax.rsqrt(jnp.mean(x * x, axis=-1, keepdims=True) + EPS)


def _prenorm_kernel(x_ref, w_ref, u_ref):
    u_ref[...] = (_rms(x_ref[...]) * w_ref[...]).astype(u_ref.dtype)


def _prenorm(x, w, rows=512):
    m, d = x.shape
    return pl.pallas_call(
        _prenorm_kernel,
        grid=(m // rows,),
        in_specs=[pl.BlockSpec((rows, d), lambda i: (i, 0)),
                  pl.BlockSpec((1, d), lambda i: (0, 0))],
        out_specs=pl.BlockSpec((rows, d), lambda i: (i, 0)),
        out_shape=jax.ShapeDtypeStruct((m, d), jnp.bfloat16),
        compiler_params=_params(("parallel",)),
        name="prenorm",
    )(x, w.reshape(1, d))


def _post_kernel(h_ref, m_ref, wpost_ref, wpre_ref, hout_ref, u_ref):
    h = h_ref[...] + _rms(m_ref[...]) * wpost_ref[...]
    hout_ref[...] = h
    u_ref[...] = (_rms(h) * wpre_ref[...]).astype(u_ref.dtype)


def _post_last_kernel(h_ref, m_ref, wpost_ref, hout_ref):
    hout_ref[...] = h_ref[...] + _rms(m_ref[...]) * wpost_ref[...]


def _post(h, m_, wpost, wpre, rows=256):
    m, d = h.shape
    row_spec = pl.BlockSpec((rows, d), lambda i: (i, 0))
    w_spec = pl.BlockSpec((1, d), lambda i: (0, 0))
    if wpre is None:
        return pl.pallas_call(
            _post_last_kernel,
            grid=(m // rows,),
            in_specs=[row_spec, row_spec, w_spec],
            out_specs=row_spec,
            out_shape=jax.ShapeDtypeStruct((m, d), jnp.float32),
            compiler_params=_params(("parallel",)),
            name="post_last",
        )(h, m_, wpost.reshape(1, d)), None
    return pl.pallas_call(
        _post_kernel,
        grid=(m // rows,),
        in_specs=[row_spec, row_spec, w_spec, w_spec],
        out_specs=[row_spec, row_spec],
        out_shape=[jax.ShapeDtypeStruct((m, d), jnp.float32),
                   jax.ShapeDtypeStruct((m, d), jnp.bfloat16)],
        compiler_params=_params(("parallel",)),
        name="post",
    )(h, m_, wpost.reshape(1, d), wpre.reshape(1, d))


def _mm_kernel(a_ref, b_ref, o_ref):
    o_ref[...] = jnp.dot(a_ref[...], b_ref[...],
                         preferred_element_type=jnp.float32).astype(o_ref.dtype)


def _mm(a, b, out_dtype, bm, bn, name):
    m, k = a.shape
    _, n = b.shape
    return pl.pallas_call(
        _mm_kernel,
        grid=(m // bm, pl.cdiv(n, bn)),
        in_specs=[pl.BlockSpec((bm, k), lambda i, j: (i, 0)),
                  pl.BlockSpec((k, bn), lambda i, j: (0, j))],
        out_specs=pl.BlockSpec((bm, bn), lambda i, j: (i, j)),
        out_shape=jax.ShapeDtypeStruct((m, n), out_dtype),
        compiler_params=_params(("parallel", "parallel")),
        name=name,
    )(a, b)


def _mm2_kernel(a1_ref, a2_ref, b1_ref, b2_ref, o_ref):
    acc = jnp.dot(a1_ref[...], b1_ref[...], preferred_element_type=jnp.float32)
    acc += jnp.dot(a2_ref[...], b2_ref[...], preferred_element_type=jnp.float32)
    o_ref[...] = acc.astype(o_ref.dtype)


def _mm2(a1, a2, b1, b2, out_dtype, bm, bn, name):
    m, k1 = a1.shape
    _, k2 = a2.shape
    _, n = b1.shape
    return pl.pallas_call(
        _mm2_kernel,
        grid=(m // bm, n // bn),
        in_specs=[pl.BlockSpec((bm, k1), lambda i, j: (i, 0)),
                  pl.BlockSpec((bm, k2), lambda i, j: (i, 0)),
                  pl.BlockSpec((k1, bn), lambda i, j: (0, j)),
                  pl.BlockSpec((k2, bn), lambda i, j: (0, j))],
        out_specs=pl.BlockSpec((bm, bn), lambda i, j: (i, j)),
        out_shape=jax.ShapeDtypeStruct((m, n), out_dtype),
        compiler_params=_params(("parallel", "parallel")),
        name=name,
    )(a1, a2, b1, b2)


def _gateup_kernel(a_ref, bg_ref, bu_ref, o_ref):
    a = a_ref[...]
    g = jnp.dot(a, bg_ref[...], preferred_element_type=jnp.float32)
    u = jnp.dot(a, bu_ref[...], preferred_element_type=jnp.float32)
    o_ref[...] = (g * jax.nn.sigmoid(g) * u).astype(o_ref.dtype)


def _gateup(a, bg, bu, bm, bn):
    m, k = a.shape
    _, n = bg.shape
    return pl.pallas_call(
        _gateup_kernel,
        grid=(m // bm, pl.cdiv(n, bn)),
        in_specs=[pl.BlockSpec((bm, k), lambda i, j: (i, 0)),
                  pl.BlockSpec((k, bn), lambda i, j: (0, j)),
                  pl.BlockSpec((k, bn), lambda i, j: (0, j))],
        out_specs=pl.BlockSpec((bm, bn), lambda i, j: (i, j)),
        out_shape=jax.ShapeDtypeStruct((m, n), jnp.bfloat16),
        compiler_params=_params(("parallel", "parallel")),
        name="ffn_gateup",
    )(a, bg, bu)


def _attn_kernel(brow_ref, q_ref, k0_ref, k1_ref, k2_ref, v0_ref, v1_ref, v2_ref,
                 o_ref, bias_ref):
    qi = pl.program_id(2)

    @pl.when(qi == 0)
    def _build_bias():
        rows = jnp.broadcast_to(brow_ref[0], (ATT_TQ, BIAS_W))
        rolled = pltpu.roll(rows, 0, 1, stride=1, stride_axis=0)
        qc = lax.broadcasted_iota(jnp.int32, (ATT_TQ, ATT_TK), 0) // CHUNK
        kc = lax.broadcasted_iota(jnp.int32, (ATT_TQ, ATT_TK), 1) // CHUNK
        in_band = (kc >= qc) & (kc <= qc + LEFT_CHUNKS)
        bias_ref[...] = jnp.where(in_band, rolled[:, :ATT_TK], NEG_INF)

    scale = A_HEAD_DIM ** -0.5
    q = (q_ref[0].astype(jnp.float32) * scale).astype(jnp.bfloat16)
    k = jnp.concatenate([k0_ref[0], k1_ref[0], k2_ref[0]], axis=0)
    v = jnp.concatenate([v0_ref[0], v1_ref[0], v2_ref[0]], axis=0)
    s = lax.dot_general(q, k, (((1,), (1,)), ((), ())),
                        preferred_element_type=jnp.float32)
    s = s + bias_ref[...]
    first_valid = (ATT_NKB - 1 - qi) * ATT_TQ
    col = lax.broadcasted_iota(jnp.int32, (ATT_TQ, ATT_TK), 1)
    s = jnp.where(col >= first_valid, s, NEG_INF)
    m = jnp.max(s, axis=-1, keepdims=True)
    p = jnp.exp(s - m)
    l = jnp.sum(p, axis=-1, keepdims=True)
    o = jnp.dot(p.astype(jnp.bfloat16), v, preferred_element_type=jnp.float32)
    o_ref[0] = (o / l).astype(o_ref.dtype)


def _band_attention(proj3, brow):
    b, s, _ = proj3.shape
    nq = s // ATT_TQ
    hq = COL_QA // A_HEAD_DIM
    hk = COL_KA // A_HEAD_DIM
    hv = COL_VA // A_HEAD_DIM

    def kv_spec(col0, back):
        return pl.BlockSpec(
            (1, ATT_TQ, A_HEAD_DIM),
            lambda bi, h, qi: (bi, jnp.maximum(qi - back, 0), col0 + h))

    return pl.pallas_call(
        _attn_kernel,
        grid=(b, A_HEADS, nq),
        in_specs=[pl.BlockSpec((1, 1, BIAS_W), lambda bi, h, qi: (h, 0, 0)),
                  pl.BlockSpec((1, ATT_TQ, A_HEAD_DIM),
                               lambda bi, h, qi: (bi, qi, hq + h)),
                  kv_spec(hk, 2), kv_spec(hk, 1), kv_spec(hk, 0),
                  kv_spec(hv, 2), kv_spec(hv, 1), kv_spec(hv, 0)],
        out_specs=pl.BlockSpec((1, ATT_TQ, A_HEAD_DIM),
                               lambda bi, h, qi: (bi, qi, h)),
        out_shape=jax.ShapeDtypeStruct((b, s, A_WIDTH), jnp.bfloat16),
        scratch_shapes=[pltpu.VMEM((ATT_TQ, ATT_TK), jnp.float32)],
        compiler_params=_params(("parallel", "parallel", "arbitrary")),
        name="band_attention",
    )(brow, proj3, proj3, proj3, proj3, proj3, proj3, proj3)


def _bias_rows(rel_bias):
    far = rel_bias[:, 2 * REL_CLIP:]
    near = rel_bias[:, :1]
    n_far = LEFT_CHUNKS * CHUNK - REL_CLIP
    n_near = ATT_TK - n_far - (2 * REL_CLIP + 1)
    row = jnp.concatenate([
        jnp.broadcast_to(far, (A_HEADS, n_far)),
        jnp.flip(rel_bias, axis=1),
        jnp.broadcast_to(near, (A_HEADS, n_near)),
        jnp.broadcast_to(far, (A_HEADS, BIAS_W - ATT_TK)),
    ], axis=1)
    return row.reshape(A_HEADS, 1, BIAS_W)


def _split3(x):
    hi = x.astype(jnp.bfloat16)
    r1 = x - hi.astype(jnp.float32)
    mid = r1.astype(jnp.bfloat16)
    r2 = r1 - mid.astype(jnp.float32)
    return hi, mid, r2.astype(jnp.bfloat16)


def _gla_kernel(q_ref, k_ref, v_ref, g_ref, fb_ref, wfg_ref, bfg_ref, nw_ref,
                o_ref, st_ref):
    c = pl.program_id(2)

    @pl.when(c == 0)
    def _reset():
        st_ref[...] = jnp.zeros_like(st_ref)

    f32 = jnp.float32
    bf16 = jnp.bfloat16
    z = jnp.dot(fb_ref[0].astype(bf16), wfg_ref[...],
                preferred_element_type=f32) + bfg_ref[...]
    log_a = (jnp.minimum(z, 0.0) - jnp.log1p(jnp.exp(-jnp.abs(z)))) / GATE_TEMP
    row = lax.broadcasted_iota(jnp.int32, (CHUNK, CHUNK), 0)
    colm = lax.broadcasted_iota(jnp.int32, (CHUNK, CHUNK), 1)
    causal = colm <= row
    tri = jnp.where(causal, 1.0, 0.0).astype(bf16)
    hi, mid, lo = _split3(log_a)
    cum = (jnp.dot(tri, hi, preferred_element_type=f32)
           + jnp.dot(tri, mid, preferred_element_type=f32)
           + jnp.dot(tri, lo, preferred_element_type=f32))
    last = cum[CHUNK - 1:CHUNK, :]

    q = q_ref[0].astype(f32) * (B_HEAD_K ** -0.5)
    k = k_ref[0].astype(f32)
    v = v_ref[0]
    q_dec = (q * jnp.exp(cum)).astype(bf16)
    k_inv = (k * jnp.exp(-cum)).astype(bf16)
    k_tail = (k * jnp.exp(last - cum)).astype(bf16)

    att = lax.dot_general(q_dec, k_inv, (((1,), (1,)), ((), ())),
                          preferred_element_type=f32)
    att = jnp.where(causal, att, 0.0).astype(bf16)
    o = jnp.dot(att, v, preferred_element_type=f32)
    st = st_ref[...]
    o += lax.dot_general(q_dec, st.astype(bf16), (((1,), (1,)), ((), ())),
                         preferred_element_type=f32)
    upd = lax.dot_general(v, k_tail, (((0,), (0,)), ((), ())),
                          preferred_element_type=f32)
    st_ref[...] = st * jnp.exp(last) + upd

    o = _rms(o) * nw_ref[...]
    g = g_ref[0].astype(f32)
    o_ref[0] = (o * (g * jax.nn.sigmoid(g))).astype(o_ref.dtype)


def _gla(proj3, fb3, wfg, bfg, nw):
    b, s, _ = proj3.shape
    nc = s // CHUNK
    cq = COL_QB // B_HEAD_K
    ck = COL_KB // B_HEAD_K
    cv = COL_VB // B_HEAD_V
    cg = COL_GB // B_HEAD_V
    return pl.pallas_call(
        _gla_kernel,
        grid=(b, B_HEADS, nc),
        in_specs=[
            pl.BlockSpec((1, CHUNK, B_HEAD_K), lambda bi, h, c: (bi, c, cq + h)),
            pl.BlockSpec((1, CHUNK, B_HEAD_K), lambda bi, h, c: (bi, c, ck + h)),
            pl.BlockSpec((1, CHUNK, B_HEAD_V), lambda bi, h, c: (bi, c, cv + h)),
            pl.BlockSpec((1, CHUNK, B_HEAD_V), lambda bi, h, c: (bi, c, cg + h)),
            pl.BlockSpec((1, CHUNK, LANES), lambda bi, h, c: (bi, c, 0)),
            pl.BlockSpec((LANES, B_HEAD_K), lambda bi, h, c: (0, h)),
            pl.BlockSpec((1, B_HEAD_K), lambda bi, h, c: (0, h)),
            pl.BlockSpec((1, B_HEAD_V), lambda bi, h, c: (0, 0)),
        ],
        out_specs=pl.BlockSpec((1, CHUNK, B_HEAD_V), lambda bi, h, c: (bi, c, h)),
        out_shape=jax.ShapeDtypeStruct((b, s, B_WIDTH), jnp.bfloat16),
        scratch_shapes=[pltpu.VMEM((B_HEAD_V, B_HEAD_K), jnp.float32)],
        compiler_params=_params(("parallel", "parallel", "arbitrary")),
        name="gla",
    )(proj3, proj3, proj3, proj3, fb3, wfg, bfg, nw)


def kernel(x, norm_mix_pre, norm_mix_post, norm_ffn_pre, norm_ffn_post, w_in, rel_bias,
           w_fgate_up, b_fgate, gla_norm, w_out, w_ffn_gate, w_ffn_up, w_ffn_down):
    b, s, d = x.shape
    depth = w_in.shape[0]
    m = b * s
    bf16 = jnp.bfloat16
    h = x.reshape(m, d)
    u = _prenorm(h, norm_mix_pre[0])
    for i in range(depth):
        w_main = w_in[i, :, :MAIN_COLS].astype(bf16)
        w_fb = jnp.pad(w_in[i, :, MAIN_COLS:], ((0, 0), (0, LANES - GATE_RANK))).astype(bf16)
        proj = _mm(u, w_main, bf16, 1024, 1024, "in_proj")
        fb = _mm(u, w_fb, jnp.float32, 1024, LANES, "fb_proj")
        proj3 = proj.reshape(b, s, MAIN_COLS)
        ya = _band_attention(proj3, _bias_rows(rel_bias[i]))
        wfg = jnp.pad(w_fgate_up[i], ((0, LANES - GATE_RANK), (0, 0))).astype(bf16)
        yb = _gla(proj3, fb.reshape(b, s, LANES), wfg,
                  b_fgate[i].reshape(1, B_KEY_WIDTH), gla_norm[i].reshape(1, B_HEAD_V))
        wo = w_out[i].astype(bf16)
        mix = _mm2(ya.reshape(m, A_WIDTH), yb.reshape(m, B_WIDTH),
                   wo[:A_WIDTH], wo[A_WIDTH:], jnp.float32, 1024, 1024, "out_proj")
        h, u = _post(h, mix, norm_mix_post[i], norm_ffn_pre[i])
        hid = _gateup(u, w_ffn_gate[i].astype(bf16), w_ffn_up[i].astype(bf16), 1024, 512)
        f = _mm(hid, w_ffn_down[i].astype(bf16), jnp.float32, 512, 512, "ffn_down")
        nxt = norm_mix_pre[i + 1] if i + 1 < depth else None
        h, u = _post(h, f, norm_ffn_post[i], nxt)
    return h.reshape(b, s, d)
```

```python
import jax
import jax.numpy as jnp
from jax import lax
from jax.experimental import pallas as pl
from jax.experimental.pallas import tpu as pltpu

D_MODEL = 4096
CHUNK = 64
LEFT_CHUNKS = 8
A_WIDTH = 2048
A_HEAD_DIM = 128
A_HEADS = A_WIDTH // A_HEAD_DIM
REL_CLIP = 128
B_WIDTH = 2048
B_HEADS = 4
B_HEAD_V = B_WIDTH // B_HEADS
B_KEY_WIDTH = B_WIDTH // 2
B_HEAD_K = B_KEY_WIDTH // B_HEADS
GATE_RANK = 16
GATE_TEMP = 16.0
EPS = 1e-6
NEG_INF = -1e30

LANES = 128
VMEM_LIMIT = 60 * 1024 * 1024

COL_QA = 0
COL_KA = A_WIDTH
COL_VA = 2 * A_WIDTH
COL_QB = 3 * A_WIDTH
COL_KB = COL_QB + B_KEY_WIDTH
COL_VB = COL_KB + B_KEY_WIDTH
COL_GB = COL_VB + B_WIDTH
MAIN_COLS = COL_GB + B_WIDTH

ATT_TQ = 256
ATT_NKB = 3
ATT_TK = ATT_NKB * ATT_TQ
ATT_HB = 4
ATT_W = ATT_HB * A_HEAD_DIM
BIAS_W = 1024

BF16 = jnp.bfloat16
F32 = jnp.float32


def _params(semantics):
    return pltpu.CompilerParams(dimension_semantics=semantics,
                                vmem_limit_bytes=VMEM_LIMIT)


def _rms(x):
    return x * lax.rsqrt(jnp.mean(x * x, axis=-1, keepdims=True) + EPS)


def _prenorm_kernel(x_ref, w_ref, u_ref):
    u_ref[...] = (_rms(x_ref[...]) * w_ref[...]).astype(u_ref.dtype)


def _prenorm(x, w, rows=512):
    m, d = x.shape
    return pl.pallas_call(
        _prenorm_kernel,
        grid=(m // rows,),
        in_specs=[pl.BlockSpec((rows, d), lambda i: (i, 0)),
                  pl.BlockSpec((1, d), lambda i: (0, 0))],
        out_specs=pl.BlockSpec((rows, d), lambda i: (i, 0)),
        out_shape=jax.ShapeDtypeStruct((m, d), BF16),
        compiler_params=_params(("parallel",)),
        name="prenorm",
    )(x, w.reshape(1, d))


def _post_kernel(h_ref, m_ref, wpost_ref, wpre_ref, hout_ref, u_ref):
    h = h_ref[...] + _rms(m_ref[...]) * wpost_ref[...]
    hout_ref[...] = h
    u_ref[...] = (_rms(h) * wpre_ref[...]).astype(u_ref.dtype)


def _post_last_kernel(h_ref, m_ref, wpost_ref, hout_ref):
    hout_ref[...] = h_ref[...] + _rms(m_ref[...]) * wpost_ref[...]


def _post(h, m_, wpost, wpre, rows=256):
    m, d = h.shape
    row_spec = pl.BlockSpec((rows, d), lambda i: (i, 0))
    w_spec = pl.BlockSpec((1, d), lambda i: (0, 0))
    if wpre is None:
        return pl.pallas_call(
            _post_last_kernel,
            grid=(m // rows,),
            in_specs=[row_spec, row_spec, w_spec],
            out_specs=row_spec,
            out_shape=jax.ShapeDtypeStruct((m, d), F32),
            compiler_params=_params(("parallel",)),
            name="post_last",
        )(h, m_, wpost.reshape(1, d)), None
    return pl.pallas_call(
        _post_kernel,
        grid=(m // rows,),
        in_specs=[row_spec, row_spec, w_spec, w_spec],
        out_specs=[row_spec, row_spec],
        out_shape=[jax.ShapeDtypeStruct((m, d), F32),
                   jax.ShapeDtypeStruct((m, d), BF16)],
        compiler_params=_params(("parallel",)),
        name="post",
    )(h, m_, wpost.reshape(1, d), wpre.reshape(1, d))


def _first_row_step():
    return pl.program_id(1) == 0


def _wmm_kernel(a_ref, w_ref, o_ref, wb_ref):
    @pl.when(_first_row_step())
    def _cast():
        wb_ref[...] = w_ref[...].astype(BF16)

    o_ref[...] = jnp.dot(a_ref[...], wb_ref[...],
                         preferred_element_type=F32).astype(o_ref.dtype)


def _wmm(a, w, layer, n_cols, out_dtype, bm, bn, name):
    m, k = a.shape
    return pl.pallas_call(
        _wmm_kernel,
        grid=(n_cols // bn, m // bm),
        in_specs=[pl.BlockSpec((bm, k), lambda j, i: (i, 0)),
                  pl.BlockSpec((None, k, bn), lambda j, i: (layer, 0, j))],
        out_specs=pl.BlockSpec((bm, bn), lambda j, i: (i, j)),
        out_shape=jax.ShapeDtypeStruct((m, n_cols), out_dtype),
        scratch_shapes=[pltpu.VMEM((k, bn), BF16)],
        compiler_params=_params(("parallel", "arbitrary")),
        name=name,
    )(a, w)


def _wmm2_kernel(a1_ref, a2_ref, w1_ref, w2_ref, o_ref, wb1_ref, wb2_ref):
    @pl.when(_first_row_step())
    def _cast():
        wb1_ref[...] = w1_ref[...].astype(BF16)
        wb2_ref[...] = w2_ref[...].astype(BF16)

    acc = jnp.dot(a1_ref[...], wb1_ref[...], preferred_element_type=F32)
    acc += jnp.dot(a2_ref[...], wb2_ref[...], preferred_element_type=F32)
    o_ref[...] = acc.astype(o_ref.dtype)


def _wmm2(a1, a2, w, layer, out_dtype, bm, bn, name):
    m, k1 = a1.shape
    _, k2 = a2.shape
    assert k1 == k2
    n = w.shape[2]
    return pl.pallas_call(
        _wmm2_kernel,
        grid=(n // bn, m // bm),
        in_specs=[pl.BlockSpec((bm, k1), lambda j, i: (i, 0)),
                  pl.BlockSpec((bm, k2), lambda j, i: (i, 0)),
                  pl.BlockSpec((None, k1, bn), lambda j, i: (layer, 0, j)),
                  pl.BlockSpec((None, k2, bn), lambda j, i: (layer, 1, j))],
        out_specs=pl.BlockSpec((bm, bn), lambda j, i: (i, j)),
        out_shape=jax.ShapeDtypeStruct((m, n), out_dtype),
        scratch_shapes=[pltpu.VMEM((k1, bn), BF16), pltpu.VMEM((k2, bn), BF16)],
        compiler_params=_params(("parallel", "arbitrary")),
        name=name,
    )(a1, a2, w, w)


def _gateup_kernel(a_ref, wg_ref, wu_ref, o_ref, wgb_ref, wub_ref):
    @pl.when(_first_row_step())
    def _cast():
        wgb_ref[...] = wg_ref[...].astype(BF16)
        wub_ref[...] = wu_ref[...].astype(BF16)

    a = a_ref[...]
    g = jnp.dot(a, wgb_ref[...], preferred_element_type=F32)
    u = jnp.dot(a, wub_ref[...], preferred_element_type=F32)
    o_ref[...] = (g * jax.nn.sigmoid(g) * u).astype(o_ref.dtype)


def _gateup(a, wg, wu, layer, bm, bn):
    m, k = a.shape
    n = wg.shape[2]
    w_spec = pl.BlockSpec((None, k, bn), lambda j, i: (layer, 0, j))
    return pl.pallas_call(
        _gateup_kernel,
        grid=(n // bn, m // bm),
        in_specs=[pl.BlockSpec((bm, k), lambda j, i: (i, 0)), w_spec, w_spec],
        out_specs=pl.BlockSpec((bm, bn), lambda j, i: (i, j)),
        out_shape=jax.ShapeDtypeStruct((m, n), BF16),
        scratch_shapes=[pltpu.VMEM((k, bn), BF16), pltpu.VMEM((k, bn), BF16)],
        compiler_params=_params(("parallel", "arbitrary")),
        name="ffn_gateup",
    )(a, wg, wu)


def _mm_kernel(a_ref, b_ref, o_ref):
    o_ref[...] = jnp.dot(a_ref[...], b_ref[...],
                         preferred_element_type=F32).astype(o_ref.dtype)


def _mm(a, b, layer, out_dtype, bm, bn, name):
    m, k = a.shape
    n = b.shape[2]
    return pl.pallas_call(
        _mm_kernel,
        grid=(m // bm, n // bn),
        in_specs=[pl.BlockSpec((bm, k), lambda i, j: (i, 0)),
                  pl.BlockSpec((None, k, bn), lambda i, j: (layer, 0, j))],
        out_specs=pl.BlockSpec((bm, bn), lambda i, j: (i, j)),
        out_shape=jax.ShapeDtypeStruct((m, n), out_dtype),
        compiler_params=_params(("parallel", "parallel")),
        name=name,
    )(a, b)


def _attn_kernel(brow_ref, q_ref, k0_ref, k1_ref, k2_ref, v0_ref, v1_ref, v2_ref,
                 o_ref, bias_ref):
    qi = pl.program_id(2)

    @pl.when(qi <= ATT_NKB - 1)
    def _build_bias():
        qc = lax.broadcasted_iota(jnp.int32, (ATT_TQ, ATT_TK), 0) // CHUNK
        col = lax.broadcasted_iota(jnp.int32, (ATT_TQ, ATT_TK), 1)
        kc = col // CHUNK
        first_valid = (ATT_NKB - 1 - qi) * ATT_TQ
        keep = (kc >= qc) & (kc <= qc + LEFT_CHUNKS) & (col >= first_valid)
        for h in range(ATT_HB):
            rows = jnp.broadcast_to(brow_ref[h], (ATT_TQ, BIAS_W))
            rolled = pltpu.roll(rows, 0, 1, stride=1, stride_axis=0)
            bias_ref[h] = jnp.where(keep, rolled[:, :ATT_TK], NEG_INF)

    scale = A_HEAD_DIM ** -0.5
    for h in range(ATT_HB):
        lanes = slice(h * A_HEAD_DIM, (h + 1) * A_HEAD_DIM)
        q = (q_ref[0, :, lanes].astype(F32) * scale).astype(BF16)
        k = jnp.concatenate([k0_ref[0, :, lanes], k1_ref[0, :, lanes],
                             k2_ref[0, :, lanes]], axis=0)
        v = jnp.concatenate([v0_ref[0, :, lanes], v1_ref[0, :, lanes],
                             v2_ref[0, :, lanes]], axis=0)
        s = lax.dot_general(q, k, (((1,), (1,)), ((), ())),
                            preferred_element_type=F32)
        s = s + bias_ref[h]
        m = jnp.max(s, axis=-1, keepdims=True)
        p = jnp.exp(s - m)
        l = jnp.sum(p, axis=-1, keepdims=True)
        o = jnp.dot(p.astype(BF16), v, preferred_element_type=F32)
        o_ref[0, :, lanes] = (o / l).astype(o_ref.dtype)


def _band_attention(proj3, brow):
    b, s, _ = proj3.shape
    nq = s // ATT_TQ
    ng = A_HEADS // ATT_HB
    gq = COL_QA // ATT_W
    gk = COL_KA // ATT_W
    gv = COL_VA // ATT_W

    def kv_spec(col0, back):
        return pl.BlockSpec(
            (1, ATT_TQ, ATT_W),
            lambda bi, g, qi: (bi, jnp.maximum(qi - back, 0), col0 + g))

    return pl.pallas_call(
        _attn_kernel,
        grid=(b, ng, nq),
        in_specs=[pl.BlockSpec((ATT_HB, 1, BIAS_W), lambda bi, g, qi: (g, 0, 0)),
                  pl.BlockSpec((1, ATT_TQ, ATT_W), lambda bi, g, qi: (bi, qi, gq + g)),
                  kv_spec(gk, 2), kv_spec(gk, 1), kv_spec(gk, 0),
                  kv_spec(gv, 2), kv_spec(gv, 1), kv_spec(gv, 0)],
        out_specs=pl.BlockSpec((1, ATT_TQ, ATT_W), lambda bi, g, qi: (bi, qi, g)),
        out_shape=jax.ShapeDtypeStruct((b, s, A_WIDTH), BF16),
        scratch_shapes=[pltpu.VMEM((ATT_HB, ATT_TQ, ATT_TK), F32)],
        compiler_params=_params(("parallel", "parallel", "arbitrary")),
        name="band_attention",
    )(brow, proj3, proj3, proj3, proj3, proj3, proj3, proj3)


def _bias_rows(rel_bias):
    far = rel_bias[:, 2 * REL_CLIP:]
    near = rel_bias[:, :1]
    n_far = LEFT_CHUNKS * CHUNK - REL_CLIP
    n_near = ATT_TK - n_far - (2 * REL_CLIP + 1)
    row = jnp.concatenate([
        jnp.broadcast_to(far, (A_HEADS, n_far)),
        jnp.flip(rel_bias, axis=1),
        jnp.broadcast_to(near, (A_HEADS, n_near)),
        jnp.broadcast_to(far, (A_HEADS, BIAS_W - ATT_TK)),
    ], axis=1)
    return row.reshape(A_HEADS, 1, BIAS_W)


def _split3(x):
    hi = x.astype(BF16)
    r1 = x - hi.astype(F32)
    mid = r1.astype(BF16)
    r2 = r1 - mid.astype(F32)
    return hi, mid, r2.astype(BF16)


def _gla_kernel(q_ref, k_ref, v_ref, g_ref, fb_ref, wfg_ref, bfg_ref, nw_ref,
                o_ref, st_ref):
    c = pl.program_id(1)

    @pl.when(c == 0)
    def _reset():
        st_ref[...] = jnp.zeros_like(st_ref)

    z = jnp.dot(fb_ref[0].astype(BF16), wfg_ref[...],
                preferred_element_type=F32) + bfg_ref[...]
    log_a = (jnp.minimum(z, 0.0) - jnp.log1p(jnp.exp(-jnp.abs(z)))) / GATE_TEMP
    row = lax.broadcasted_iota(jnp.int32, (CHUNK, CHUNK), 0)
    colm = lax.broadcasted_iota(jnp.int32, (CHUNK, CHUNK), 1)
    causal = colm <= row
    tri = jnp.where(causal, 1.0, 0.0).astype(BF16)
    hi, mid, lo = _split3(log_a)
    cum = (jnp.dot(tri, hi, preferred_element_type=F32)
           + jnp.dot(tri, mid, preferred_element_type=F32)
           + jnp.dot(tri, lo, preferred_element_type=F32))
    last = cum[CHUNK - 1:CHUNK, :]
    e_cum = jnp.exp(cum)
    e_neg = jnp.exp(-cum)
    e_tail = jnp.exp(last - cum)
    e_last = jnp.exp(last)

    q_all = q_ref[0].astype(F32) * (B_HEAD_K ** -0.5)
    k_all = k_ref[0].astype(F32)
    q_dec_all = (q_all * e_cum).astype(BF16)
    k_inv_all = (k_all * e_neg).astype(BF16)
    k_tail_all = (k_all * e_tail).astype(BF16)

    for h in range(B_HEADS):
        kl = slice(h * B_HEAD_K, (h + 1) * B_HEAD_K)
        vl = slice(h * B_HEAD_V, (h + 1) * B_HEAD_V)
        q_dec = q_dec_all[:, kl]
        v = v_ref[0, :, vl]
        att = lax.dot_general(q_dec, k_inv_all[:, kl], (((1,), (1,)), ((), ())),
                              preferred_element_type=F32)
        att = jnp.where(causal, att, 0.0).astype(BF16)
        o = jnp.dot(att, v, preferred_element_type=F32)
        st = st_ref[h]
        o += lax.dot_general(q_dec, st.astype(BF16), (((1,), (1,)), ((), ())),
                             preferred_element_type=F32)
        upd = lax.dot_general(v, k_tail_all[:, kl], (((0,), (0,)), ((), ())),
                              preferred_element_type=F32)
        st_ref[h] = st * e_last[:, kl] + upd

        o = _rms(o) * nw_ref[...]
        g = g_ref[0, :, vl].astype(F32)
        o_ref[0, :, vl] = (o * (g * jax.nn.sigmoid(g))).astype(o_ref.dtype)


def _gla(proj3, fb3, wfg, bfg, nw):
    b, s, _ = proj3.shape
    nc = s // CHUNK
    cq = COL_QB // B_KEY_WIDTH
    ck = COL_KB // B_KEY_WIDTH
    cv = COL_VB // B_WIDTH
    cg = COL_GB // B_WIDTH
    return pl.pallas_call(
        _gla_kernel,
        grid=(b, nc),
        in_specs=[
            pl.BlockSpec((1, CHUNK, B_KEY_WIDTH), lambda bi, c: (bi, c, cq)),
            pl.BlockSpec((1, CHUNK, B_KEY_WIDTH), lambda bi, c: (bi, c, ck)),
            pl.BlockSpec((1, CHUNK, B_WIDTH), lambda bi, c: (bi, c, cv)),
            pl.BlockSpec((1, CHUNK, B_WIDTH), lambda bi, c: (bi, c, cg)),
            pl.BlockSpec((1, CHUNK, LANES), lambda bi, c: (bi, c, 0)),
            pl.BlockSpec((LANES, B_KEY_WIDTH), lambda bi, c: (0, 0)),
            pl.BlockSpec((1, B_KEY_WIDTH), lambda bi, c: (0, 0)),
            pl.BlockSpec((1, B_HEAD_V), lambda bi, c: (0, 0)),
        ],
        out_specs=pl.BlockSpec((1, CHUNK, B_WIDTH), lambda bi, c: (bi, c, 0)),
        out_shape=jax.ShapeDtypeStruct((b, s, B_WIDTH), BF16),
        scratch_shapes=[pltpu.VMEM((B_HEADS, B_HEAD_V, B_HEAD_K), F32)],
        compiler_params=_params(("parallel", "arbitrary")),
        name="gla",
    )(proj3, proj3, proj3, proj3, fb3, wfg, bfg, nw)


def kernel(x, norm_mix_pre, norm_mix_post, norm_ffn_pre, norm_ffn_post, w_in, rel_bias,
           w_fgate_up, b_fgate, gla_norm, w_out, w_ffn_gate, w_ffn_up, w_ffn_down):
    b, s, d = x.shape
    depth = w_in.shape[0]
    m = b * s
    pad_rank = LANES - GATE_RANK
    w_fb = jnp.pad(w_in[:, :, MAIN_COLS:], ((0, 0), (0, 0), (0, pad_rank))).astype(BF16)
    w_fg = jnp.pad(w_fgate_up, ((0, 0), (0, pad_rank), (0, 0))).astype(BF16)
    w_down = w_ffn_down.astype(BF16)
    h = x.reshape(m, d)
    u = _prenorm(h, norm_mix_pre[0])
    for i in range(depth):
        proj = _wmm(u, w_in, i, MAIN_COLS, BF16, 1024, 768, "in_proj")
        fb = _mm(u, w_fb, i, F32, 1024, LANES, "fb_proj")
        proj3 = proj.reshape(b, s, MAIN_COLS)
        ya = _band_attention(proj3, _bias_rows(rel_bias[i]))
        yb = _gla(proj3, fb.reshape(b, s, LANES), w_fg[i],
                  b_fgate[i].reshape(1, B_KEY_WIDTH), gla_norm[i].reshape(1, B_HEAD_V))
        mix = _wmm2(ya.reshape(m, A_WIDTH), yb.reshape(m, B_WIDTH), w_out, i,
                    F32, 1024, 512, "out_proj")
        h, u = _post(h, mix, norm_mix_post[i], norm_ffn_pre[i])
        hid = _gateup(u, w_ffn_gate, w_ffn_up, i, 1024, 256)
        f = _mm(hid, w_down, i, F32, 512, 512, "ffn_down")
        nxt = norm_mix_pre[i + 1] if i + 1 < depth else None
        h, u = _post(h, f, norm_ffn_post[i], nxt)
    return h.reshape(b, s, d)
```

```python
from typing import NamedTuple

import jax
import jax.numpy as jnp
from jax import lax
from jax.experimental import pallas as pl
from jax.experimental.pallas import tpu as pltpu

D_MODEL = 4096
CHUNK = 64
LEFT_CHUNKS = 8
A_WIDTH = 2048
A_HEAD_DIM = 128
A_HEADS = A_WIDTH // A_HEAD_DIM
REL_CLIP = 128
B_WIDTH = 2048
B_HEADS = 4
B_HEAD_V = B_WIDTH // B_HEADS
B_KEY_WIDTH = B_WIDTH // 2
B_HEAD_K = B_KEY_WIDTH // B_HEADS
GATE_RANK = 16
GATE_TEMP = 16.0
EPS = 1e-6
NEG_INF = -1e30
LOG2E = 1.4426950408889634

LANES = 128
VMEM_LIMIT = 60 * 1024 * 1024

COL_QA = 0
COL_KA = A_WIDTH
COL_VA = 2 * A_WIDTH
COL_QB = 3 * A_WIDTH
COL_KB = COL_QB + B_KEY_WIDTH
COL_VB = COL_KB + B_KEY_WIDTH
COL_GB = COL_VB + B_WIDTH
MAIN_COLS = COL_GB + B_WIDTH

ATT_TQ = 256
ATT_NKB = 3
ATT_TK = ATT_NKB * ATT_TQ
ATT_HB = 4
ATT_W = ATT_HB * A_HEAD_DIM
BIAS_W = 1024

BF16 = jnp.bfloat16
F32 = jnp.float32


class Passenger(NamedTuple):
    src: jax.Array
    layer: int
    slabs: int
    rows: int


def _call(body, *, grid, in_specs, out_specs, out_shape, args, name,
          scratch_shapes=(), passengers=()):
    n_in, n_out, n_pass = len(in_specs), len(out_specs), len(passengers)
    strides = [1] * len(grid)
    for ax in range(len(grid) - 2, -1, -1):
        strides[ax] = strides[ax + 1] * grid[ax + 1]
    n_steps = strides[0] * grid[0]

    in_specs, out_specs, out_shape, args = (list(in_specs), list(out_specs),
                                            list(out_shape), list(args))
    pass_in, pass_out, pass_shape = [], [], []
    for p in passengers:
        rows, cols = p.rows, p.src.shape[2]
        assert p.slabs <= n_steps and rows % p.slabs == 0 and rows <= p.src.shape[1]
        slab = rows // p.slabs

        def slab_of(*ids, last=p.slabs - 1):
            step = sum(i * s for i, s in zip(ids, strides))
            return jnp.minimum(step, last)

        pass_in.append(pl.BlockSpec(
            (None, slab, cols),
            lambda *ids, f=slab_of, layer=p.layer: (layer, f(*ids), 0)))
        pass_out.append(pl.BlockSpec((slab, cols), lambda *ids, f=slab_of: (f(*ids), 0)))
        pass_shape.append(jax.ShapeDtypeStruct((rows, cols), BF16))

    def kern(*refs):
        ins = refs[:n_in]
        srcs = refs[n_in:n_in + n_pass]
        outs = refs[n_in + n_pass:n_in + n_pass + n_out]
        dsts = refs[n_in + n_pass + n_out:n_in + 2 * n_pass + n_out]
        scratch = refs[n_in + 2 * n_pass + n_out:]
        for s_ref, d_ref in zip(srcs, dsts):
            d_ref[...] = s_ref[...].astype(BF16)
        body(*ins, *outs, *scratch)

    res = pl.pallas_call(
        kern,
        grid=grid,
        in_specs=in_specs + pass_in,
        out_specs=out_specs + pass_out,
        out_shape=out_shape + pass_shape,
        scratch_shapes=list(scratch_shapes),
        compiler_params=pltpu.CompilerParams(
            dimension_semantics=("arbitrary",) * len(grid),
            vmem_limit_bytes=VMEM_LIMIT),
        name=name,
    )(*args, *[p.src for p in passengers])
    return res[:n_out], res[n_out:]


def _rms(x):
    return x * lax.rsqrt(jnp.mean(x * x, axis=-1, keepdims=True) + EPS)


def _prenorm_body(x_ref, w_ref, u_ref):
    u_ref[...] = (_rms(x_ref[...]) * w_ref[...]).astype(u_ref.dtype)


def _prenorm(x, w, rows=512):
    m, d = x.shape
    (u,), _ = _call(
        _prenorm_body,
        grid=(m // rows,),
        in_specs=[pl.BlockSpec((rows, d), lambda i: (i, 0)),
                  pl.BlockSpec((1, d), lambda i: (0, 0))],
        out_specs=[pl.BlockSpec((rows, d), lambda i: (i, 0))],
        out_shape=[jax.ShapeDtypeStruct((m, d), BF16)],
        args=[x, w.reshape(1, d)],
        name="prenorm")
    return u


def _post_body(h_ref, m_ref, wpost_ref, wpre_ref, hout_ref, u_ref):
    h = h_ref[...] + _rms(m_ref[...]) * wpost_ref[...]
    hout_ref[...] = h
    u_ref[...] = (_rms(h) * wpre_ref[...]).astype(u_ref.dtype)


def _post_last_body(h_ref, m_ref, wpost_ref, hout_ref):
    hout_ref[...] = h_ref[...] + _rms(m_ref[...]) * wpost_ref[...]


def _post(h, m_, wpost, wpre, rows=256):
    m, d = h.shape
    row_spec = pl.BlockSpec((rows, d), lambda i: (i, 0))
    w_spec = pl.BlockSpec((1, d), lambda i: (0, 0))
    if wpre is None:
        (h_new,), _ = _call(
            _post_last_body, grid=(m // rows,),
            in_specs=[row_spec, row_spec, w_spec], out_specs=[row_spec],
            out_shape=[jax.ShapeDtypeStruct((m, d), F32)],
            args=[h, m_, wpost.reshape(1, d)], name="post_last")
        return h_new, None
    (h_new, u), _ = _call(
        _post_body, grid=(m // rows,),
        in_specs=[row_spec, row_spec, w_spec, w_spec], out_specs=[row_spec, row_spec],
        out_shape=[jax.ShapeDtypeStruct((m, d), F32), jax.ShapeDtypeStruct((m, d), BF16)],
        args=[h, m_, wpost.reshape(1, d), wpre.reshape(1, d)], name="post")
    return h_new, u


def _round_weight(passenger):
    _, (wb,) = _call(lambda: None, grid=(passenger.slabs,), in_specs=[], out_specs=[],
                     out_shape=[], args=[], name="round_weight", passengers=[passenger])
    return wb


_NT = (((1,), (1,)), ((), ()))


def _mm_body(a_ref, b_ref, o_ref):
    o_ref[...] = jnp.dot(a_ref[...], b_ref[...],
                         preferred_element_type=F32).astype(o_ref.dtype)


def _mm(a, b, out_dtype, bm, bn, name, passengers=()):
    m, k = a.shape
    n = b.shape[1]
    return _call(
        _mm_body,
        grid=(m // bm, n // bn),
        in_specs=[pl.BlockSpec((bm, k), lambda i, j: (i, 0)),
                  pl.BlockSpec((k, bn), lambda i, j: (0, j))],
        out_specs=[pl.BlockSpec((bm, bn), lambda i, j: (i, j))],
        out_shape=[jax.ShapeDtypeStruct((m, n), out_dtype)],
        args=[a, b], name=name, passengers=passengers)


def _mm_nt_body(a_ref, bt_ref, o_ref):
    o_ref[...] = lax.dot_general(a_ref[...], bt_ref[...], _NT,
                                 preferred_element_type=F32).astype(o_ref.dtype)


def _mm_nt(a, bt, out_dtype, bm, bn, name, passengers=()):
    m, k = a.shape
    n = bt.shape[0]
    return _call(
        _mm_nt_body,
        grid=(m // bm, n // bn),
        in_specs=[pl.BlockSpec((bm, k), lambda i, j: (i, 0)),
                  pl.BlockSpec((bn, k), lambda i, j: (j, 0))],
        out_specs=[pl.BlockSpec((bm, bn), lambda i, j: (i, j))],
        out_shape=[jax.ShapeDtypeStruct((m, n), out_dtype)],
        args=[a, bt], name=name, passengers=passengers)


def _fb_body(a_ref, wt_ref, o_ref):
    wt = wt_ref[...].astype(BF16)
    wt = jnp.concatenate(
        [wt, jnp.zeros((LANES - GATE_RANK, wt.shape[1]), BF16)], axis=0)
    o_ref[...] = lax.dot_general(a_ref[...], wt, _NT, preferred_element_type=F32)


def _fb_proj(a, w_t, layer, bm):
    m, k = a.shape
    (fb,), _ = _call(
        _fb_body,
        grid=(m // bm,),
        in_specs=[pl.BlockSpec((bm, k), lambda i: (i, 0)),
                  pl.BlockSpec((None, GATE_RANK, k),
                               lambda i: (layer, MAIN_COLS // GATE_RANK, 0))],
        out_specs=[pl.BlockSpec((bm, LANES), lambda i: (i, 0))],
        out_shape=[jax.ShapeDtypeStruct((m, LANES), F32)],
        args=[a, w_t], name="fb_proj")
    return fb


def _mm2_body(a1_ref, a2_ref, b1_ref, b2_ref, o_ref):
    acc = jnp.dot(a1_ref[...], b1_ref[...], preferred_element_type=F32)
    acc += jnp.dot(a2_ref[...], b2_ref[...], preferred_element_type=F32)
    o_ref[...] = acc.astype(o_ref.dtype)


def _mm2(a1, a2, b, out_dtype, bm, bn, name):
    m, k1 = a1.shape
    _, k2 = a2.shape
    assert k1 == k2
    n = b.shape[1]
    (o,), _ = _call(
        _mm2_body,
        grid=(m // bm, n // bn),
        in_specs=[pl.BlockSpec((bm, k1), lambda i, j: (i, 0)),
                  pl.BlockSpec((bm, k2), lambda i, j: (i, 0)),
                  pl.BlockSpec((k1, bn), lambda i, j: (0, j)),
                  pl.BlockSpec((k2, bn), lambda i, j: (1, j))],
        out_specs=[pl.BlockSpec((bm, bn), lambda i, j: (i, j))],
        out_shape=[jax.ShapeDtypeStruct((m, n), out_dtype)],
        args=[a1, a2, b, b], name=name)
    return o


def _gateup_body(a_ref, bg_ref, bu_ref, o_ref):
    a = a_ref[...]
    g = jnp.dot(a, bg_ref[...], preferred_element_type=F32)
    u = jnp.dot(a, bu_ref[...], preferred_element_type=F32)
    o_ref[...] = (g * jax.nn.sigmoid(g) * u).astype(o_ref.dtype)


def _gateup(a, bg, bu, bm, bn, passengers=()):
    m, k = a.shape
    n = bg.shape[1]
    w_spec = pl.BlockSpec((k, bn), lambda i, j: (0, j))
    return _call(
        _gateup_body,
        grid=(m // bm, pl.cdiv(n, bn)),
        in_specs=[pl.BlockSpec((bm, k), lambda i, j: (i, 0)), w_spec, w_spec],
        out_specs=[pl.BlockSpec((bm, bn), lambda i, j: (i, j))],
        out_shape=[jax.ShapeDtypeStruct((m, n), BF16)],
        args=[a, bg, bu], name="ffn_gateup", passengers=passengers)


def _attn_body(brow_ref, q_ref, k0_ref, k1_ref, k2_ref, v0_ref, v1_ref, v2_ref,
               o_ref, bias_ref):
    qi = pl.program_id(2)

    @pl.when(qi <= ATT_NKB - 1)
    def _build_bias():
        qc = lax.broadcasted_iota(jnp.int32, (ATT_TQ, ATT_TK), 0) // CHUNK
        col = lax.broadcasted_iota(jnp.int32, (ATT_TQ, ATT_TK), 1)
        kc = col // CHUNK
        first_valid = (ATT_NKB - 1 - qi) * ATT_TQ
        keep = (kc >= qc) & (kc <= qc + LEFT_CHUNKS) & (col >= first_valid)
        for h in range(ATT_HB):
            rows = jnp.broadcast_to(brow_ref[h] * LOG2E, (ATT_TQ, BIAS_W))
            rolled = pltpu.roll(rows, 0, 1, stride=1, stride_axis=0)
            bias_ref[h] = jnp.where(keep, rolled[:, :ATT_TK], NEG_INF)

    scale = A_HEAD_DIM ** -0.5 * LOG2E
    for h in range(ATT_HB):
        lanes = slice(h * A_HEAD_DIM, (h + 1) * A_HEAD_DIM)
        q = (q_ref[0, :, lanes].astype(F32) * scale).astype(BF16)
        k = jnp.concatenate([k0_ref[0, :, lanes], k1_ref[0, :, lanes],
                             k2_ref[0, :, lanes]], axis=0)
        v = jnp.concatenate([v0_ref[0, :, lanes], v1_ref[0, :, lanes],
                             v2_ref[0, :, lanes]], axis=0)
        s = lax.dot_general(q, k, (((1,), (1,)), ((), ())),
                            preferred_element_type=F32)
        s = s + bias_ref[h]
        m = jnp.max(s, axis=-1, keepdims=True)
        p = jnp.exp2(s - m)
        l = jnp.sum(p, axis=-1, keepdims=True)
        o = jnp.dot(p.astype(BF16), v, preferred_element_type=F32)
        o_ref[0, :, lanes] = (o / l).astype(o_ref.dtype)


def _band_attention(proj3, brow, passengers=()):
    b, s, _ = proj3.shape
    nq = s // ATT_TQ
    ng = A_HEADS // ATT_HB
    gq = COL_QA // ATT_W
    gk = COL_KA // ATT_W
    gv = COL_VA // ATT_W

    def kv_spec(col0, back):
        return pl.BlockSpec(
            (1, ATT_TQ, ATT_W),
            lambda bi, g, qi: (bi, jnp.maximum(qi - back, 0), col0 + g))

    return _call(
        _attn_body,
        grid=(b, ng, nq),
        in_specs=[pl.BlockSpec((ATT_HB, 1, BIAS_W), lambda bi, g, qi: (g, 0, 0)),
                  pl.BlockSpec((1, ATT_TQ, ATT_W), lambda bi, g, qi: (bi, qi, gq + g)),
                  kv_spec(gk, 2), kv_spec(gk, 1), kv_spec(gk, 0),
                  kv_spec(gv, 2), kv_spec(gv, 1), kv_spec(gv, 0)],
        out_specs=[pl.BlockSpec((1, ATT_TQ, ATT_W), lambda bi, g, qi: (bi, qi, g))],
        out_shape=[jax.ShapeDtypeStruct((b, s, A_WIDTH), BF16)],
        scratch_shapes=[pltpu.VMEM((ATT_HB, ATT_TQ, ATT_TK), F32)],
        args=[brow] + [proj3] * 7, name="band_attention", passengers=passengers)


def _bias_rows(rel_bias):
    far = rel_bias[:, 2 * REL_CLIP:]
    near = rel_bias[:, :1]
    n_far = LEFT_CHUNKS * CHUNK - REL_CLIP
    n_near = ATT_TK - n_far - (2 * REL_CLIP + 1)
    row = jnp.concatenate([
        jnp.broadcast_to(far, (A_HEADS, n_far)),
        jnp.flip(rel_bias, axis=1),
        jnp.broadcast_to(near, (A_HEADS, n_near)),
        jnp.broadcast_to(far, (A_HEADS, BIAS_W - ATT_TK)),
    ], axis=1)
    return row.reshape(A_HEADS, 1, BIAS_W)


def _split3(x):
    hi = x.astype(BF16)
    r1 = x - hi.astype(F32)
    mid = r1.astype(BF16)
    r2 = r1 - mid.astype(F32)
    return hi, mid, r2.astype(BF16)


def _gla_body(q_ref, k_ref, v_ref, g_ref, fb_ref, wfg_ref, bfg_ref, nw_ref,
              o_ref, st_ref):
    c = pl.program_id(1)

    @pl.when(c == 0)
    def _reset():
        st_ref[...] = jnp.zeros_like(st_ref)

    z = jnp.dot(fb_ref[0].astype(BF16), wfg_ref[...],
                preferred_element_type=F32) + bfg_ref[...]
    log_a = (jnp.minimum(z, 0.0) - jnp.log1p(jnp.exp(-jnp.abs(z)))) / GATE_TEMP
    row = lax.broadcasted_iota(jnp.int32, (CHUNK, CHUNK), 0)
    colm = lax.broadcasted_iota(jnp.int32, (CHUNK, CHUNK), 1)
    causal = colm <= row
    tri = jnp.where(causal, 1.0, 0.0).astype(BF16)
    hi, mid, lo = _split3(log_a)
    cum = (jnp.dot(tri, hi, preferred_element_type=F32)
           + jnp.dot(tri, mid, preferred_element_type=F32)
           + jnp.dot(tri, lo, preferred_element_type=F32))
    last = cum[CHUNK - 1:CHUNK, :]
    e_cum = jnp.exp(cum)
    e_neg = jnp.exp(-cum)
    e_tail = jnp.exp(last - cum)
    e_last = jnp.exp(last)

    q_all = q_ref[0].astype(F32) * (B_HEAD_K ** -0.5)
    k_all = k_ref[0].astype(F32)
    q_dec_all = (q_all * e_cum).astype(BF16)
    k_inv_all = (k_all * e_neg).astype(BF16)
    k_tail_all = (k_all * e_tail).astype(BF16)

    for h in range(B_HEADS):
        kl = slice(h * B_HEAD_K, (h + 1) * B_HEAD_K)
        vl = slice(h * B_HEAD_V, (h + 1) * B_HEAD_V)
        q_dec = q_dec_all[:, kl]
        v = v_ref[0, :, vl]
        att = lax.dot_general(q_dec, k_inv_all[:, kl], (((1,), (1,)), ((), ())),
                              preferred_element_type=F32)
        att = jnp.where(causal, att, 0.0).astype(BF16)
        o = jnp.dot(att, v, preferred_element_type=F32)
        st = st_ref[h]
        o += lax.dot_general(q_dec, st.astype(BF16), (((1,), (1,)), ((), ())),
                             preferred_element_type=F32)
        upd = lax.dot_general(v, k_tail_all[:, kl], (((0,), (0,)), ((), ())),
                              preferred_element_type=F32)
        st_ref[h] = st * e_last[:, kl] + upd

        o = _rms(o) * nw_ref[...]
        g = g_ref[0, :, vl].astype(F32)
        o_ref[0, :, vl] = (o * (g * jax.nn.sigmoid(g))).astype(o_ref.dtype)


def _gla(proj3, fb3, wfg, bfg, nw, passengers=()):
    b, s, _ = proj3.shape
    nc = s // CHUNK
    cq = COL_QB // B_KEY_WIDTH
    ck = COL_KB // B_KEY_WIDTH
    cv = COL_VB // B_WIDTH
    cg = COL_GB // B_WIDTH
    return _call(
        _gla_body,
        grid=(b, nc),
        in_specs=[
            pl.BlockSpec((1, CHUNK, B_KEY_WIDTH), lambda bi, c: (bi, c, cq)),
            pl.BlockSpec((1, CHUNK, B_KEY_WIDTH), lambda bi, c: (bi, c, ck)),
            pl.BlockSpec((1, CHUNK, B_WIDTH), lambda bi, c: (bi, c, cv)),
            pl.BlockSpec((1, CHUNK, B_WIDTH), lambda bi, c: (bi, c, cg)),
            pl.BlockSpec((1, CHUNK, LANES), lambda bi, c: (bi, c, 0)),
            pl.BlockSpec((LANES, B_KEY_WIDTH), lambda bi, c: (0, 0)),
            pl.BlockSpec((1, B_KEY_WIDTH), lambda bi, c: (0, 0)),
            pl.BlockSpec((1, B_HEAD_V), lambda bi, c: (0, 0)),
        ],
        out_specs=[pl.BlockSpec((1, CHUNK, B_WIDTH), lambda bi, c: (bi, c, 0))],
        out_shape=[jax.ShapeDtypeStruct((b, s, B_WIDTH), BF16)],
        scratch_shapes=[pltpu.VMEM((B_HEADS, B_HEAD_V, B_HEAD_K), F32)],
        args=[proj3, proj3, proj3, proj3, fb3, wfg, bfg, nw], name="gla",
        passengers=passengers)


def kernel(x, norm_mix_pre, norm_mix_post, norm_ffn_pre, norm_ffn_post, w_in, rel_bias,
           w_fgate_up, b_fgate, gla_norm, w_out, w_ffn_gate, w_ffn_up, w_ffn_down):
    b, s, d = x.shape
    depth = w_in.shape[0]
    m = b * s
    hidden = w_ffn_down.shape[1]
    w_fg = jnp.pad(w_fgate_up, ((0, 0), (0, LANES - GATE_RANK), (0, 0))).astype(BF16)
    w_in_t = jnp.swapaxes(w_in, 1, 2)
    h = x.reshape(m, d)
    u = _prenorm(h, norm_mix_pre[0])
    win_b = _round_weight(Passenger(w_in_t, 0, 128, MAIN_COLS))
    for i in range(depth):
        (proj,), (wgate_b,) = _mm_nt(u, win_b, BF16, 1024, 768, "in_proj",
                                     passengers=[Passenger(w_ffn_gate, i, 128, d)])
        fb = _fb_proj(u, w_in_t, i, 1024)
        proj3 = proj.reshape(b, s, MAIN_COLS)
        (ya,), (wup_b,) = _band_attention(proj3, _bias_rows(rel_bias[i]),
                                          passengers=[Passenger(w_ffn_up, i, 128, d)])
        (yb,), (wout_b,) = _gla(proj3, fb.reshape(b, s, LANES), w_fg[i],
                                b_fgate[i].reshape(1, B_KEY_WIDTH),
                                gla_norm[i].reshape(1, B_HEAD_V),
                                passengers=[Passenger(w_out, i, 128, d)])
        mix = _mm2(ya.reshape(m, A_WIDTH), yb.reshape(m, B_WIDTH), wout_b,
                   F32, 1024, 1024, "out_proj")
        h, u = _post(h, mix, norm_mix_post[i], norm_ffn_pre[i])
        (hid,), (wdown_b,) = _gateup(
            u, wgate_b, wup_b, 1024, 512,
            passengers=[Passenger(w_ffn_down, i, hidden // 128, hidden)])
        nxt = [Passenger(w_in_t, i + 1, 128, MAIN_COLS)] if i + 1 < depth else []
        (f,), nxt_b = _mm(hid, wdown_b, F32, 512, 512, "ffn_down", passengers=nxt)
        if nxt_b:
            win_b = nxt_b[0]
        h, u = _post(h, f, norm_ffn_post[i], norm_mix_pre[i + 1] if i + 1 < depth else None)
    return h.reshape(b, s, d)
```

```python
from typing import NamedTuple

import jax
import jax.numpy as jnp
from jax import lax
from jax.experimental import pallas as pl
from jax.experimental.pallas import tpu as pltpu

D_MODEL = 4096
CHUNK = 64
LEFT_CHUNKS = 8
A_WIDTH = 2048
A_HEAD_DIM = 128
A_HEADS = A_WIDTH // A_HEAD_DIM
REL_CLIP = 128
B_WIDTH = 2048
B_HEADS = 4
B_HEAD_V = B_WIDTH // B_HEADS
B_KEY_WIDTH = B_WIDTH // 2
B_HEAD_K = B_KEY_WIDTH // B_HEADS
GATE_RANK = 16
GATE_TEMP = 16.0
EPS = 1e-6
NEG_INF = -1e30
LOG2E = 1.4426950408889634

LANES = 128
VMEM_LIMIT = 60 * 1024 * 1024

COL_QA = 0
COL_KA = A_WIDTH
COL_VA = 2 * A_WIDTH
COL_QB = 3 * A_WIDTH
COL_KB = COL_QB + B_KEY_WIDTH
COL_VB = COL_KB + B_KEY_WIDTH
COL_GB = COL_VB + B_WIDTH
MAIN_COLS = COL_GB + B_WIDTH

ATT_TQ = 256
ATT_NKB = 3
ATT_TK = ATT_NKB * ATT_TQ
ATT_HB = 8
ATT_W = ATT_HB * A_HEAD_DIM
BIAS_W = 1024
GLA_ROWS = 256

BF16 = jnp.bfloat16
F32 = jnp.float32


class Passenger(NamedTuple):
    src: jax.Array
    layer: int
    slabs: int
    rows: int


def _call(body, *, grid, in_specs, out_specs, out_shape, args, name,
          scratch_shapes=(), passengers=()):
    n_in, n_out, n_pass = len(in_specs), len(out_specs), len(passengers)
    strides = [1] * len(grid)
    for ax in range(len(grid) - 2, -1, -1):
        strides[ax] = strides[ax + 1] * grid[ax + 1]
    n_steps = strides[0] * grid[0]

    in_specs, out_specs, out_shape, args = (list(in_specs), list(out_specs),
                                            list(out_shape), list(args))
    pass_in, pass_out, pass_shape = [], [], []
    for p in passengers:
        rows, cols = p.rows, p.src.shape[2]
        assert p.slabs <= n_steps and rows % p.slabs == 0 and rows <= p.src.shape[1]
        slab = rows // p.slabs

        def slab_of(*ids, last=p.slabs - 1):
            step = sum(i * s for i, s in zip(ids, strides))
            return jnp.minimum(step, last)

        pass_in.append(pl.BlockSpec(
            (None, slab, cols),
            lambda *ids, f=slab_of, layer=p.layer: (layer, f(*ids), 0)))
        pass_out.append(pl.BlockSpec((slab, cols), lambda *ids, f=slab_of: (f(*ids), 0)))
        pass_shape.append(jax.ShapeDtypeStruct((rows, cols), BF16))

    def kern(*refs):
        ins = refs[:n_in]
        srcs = refs[n_in:n_in + n_pass]
        outs = refs[n_in + n_pass:n_in + n_pass + n_out]
        dsts = refs[n_in + n_pass + n_out:n_in + 2 * n_pass + n_out]
        scratch = refs[n_in + 2 * n_pass + n_out:]
        for s_ref, d_ref in zip(srcs, dsts):
            d_ref[...] = s_ref[...].astype(BF16)
        body(*ins, *outs, *scratch)

    res = pl.pallas_call(
        kern,
        grid=grid,
        in_specs=in_specs + pass_in,
        out_specs=out_specs + pass_out,
        out_shape=out_shape + pass_shape,
        scratch_shapes=list(scratch_shapes),
        compiler_params=pltpu.CompilerParams(
            dimension_semantics=("arbitrary",) * len(grid),
            vmem_limit_bytes=VMEM_LIMIT),
        name=name,
    )(*args, *[p.src for p in passengers])
    return res[:n_out], res[n_out:]


def _rms(x):
    return x * lax.rsqrt(jnp.mean(x * x, axis=-1, keepdims=True) + EPS)


def _prenorm_body(x_ref, w_ref, u_ref):
    u_ref[...] = (_rms(x_ref[...]) * w_ref[...]).astype(u_ref.dtype)


def _prenorm(x, w, rows=512):
    m, d = x.shape
    (u,), _ = _call(
        _prenorm_body,
        grid=(m // rows,),
        in_specs=[pl.BlockSpec((rows, d), lambda i: (i, 0)),
                  pl.BlockSpec((1, d), lambda i: (0, 0))],
        out_specs=[pl.BlockSpec((rows, d), lambda i: (i, 0))],
        out_shape=[jax.ShapeDtypeStruct((m, d), BF16)],
        args=[x, w.reshape(1, d)],
        name="prenorm")
    return u


def _post_body(h_ref, m_ref, wpost_ref, wpre_ref, hout_ref, u_ref):
    h = h_ref[...] + _rms(m_ref[...]) * wpost_ref[...]
    hout_ref[...] = h
    u_ref[...] = (_rms(h) * wpre_ref[...]).astype(u_ref.dtype)


def _post_last_body(h_ref, m_ref, wpost_ref, hout_ref):
    hout_ref[...] = h_ref[...] + _rms(m_ref[...]) * wpost_ref[...]


def _post(h, m_, wpost, wpre, rows=256):
    m, d = h.shape
    row_spec = pl.BlockSpec((rows, d), lambda i: (i, 0))
    w_spec = pl.BlockSpec((1, d), lambda i: (0, 0))
    if wpre is None:
        (h_new,), _ = _call(
            _post_last_body, grid=(m // rows,),
            in_specs=[row_spec, row_spec, w_spec], out_specs=[row_spec],
            out_shape=[jax.ShapeDtypeStruct((m, d), F32)],
            args=[h, m_, wpost.reshape(1, d)], name="post_last")
        return h_new, None
    (h_new, u), _ = _call(
        _post_body, grid=(m // rows,),
        in_specs=[row_spec, row_spec, w_spec, w_spec], out_specs=[row_spec, row_spec],
        out_shape=[jax.ShapeDtypeStruct((m, d), F32), jax.ShapeDtypeStruct((m, d), BF16)],
        args=[h, m_, wpost.reshape(1, d), wpre.reshape(1, d)], name="post")
    return h_new, u


def _round_weight(passenger):
    _, (wb,) = _call(lambda: None, grid=(passenger.slabs,), in_specs=[], out_specs=[],
                     out_shape=[], args=[], name="round_weight", passengers=[passenger])
    return wb


_NT = (((1,), (1,)), ((), ()))


def _mm_body(a_ref, b_ref, o_ref):
    o_ref[...] = jnp.dot(a_ref[...], b_ref[...],
                         preferred_element_type=F32).astype(o_ref.dtype)


def _mm(a, b, out_dtype, bm, bn, name, passengers=()):
    m, k = a.shape
    n = b.shape[1]
    return _call(
        _mm_body,
        grid=(m // bm, n // bn),
        in_specs=[pl.BlockSpec((bm, k), lambda i, j: (i, 0)),
                  pl.BlockSpec((k, bn), lambda i, j: (0, j))],
        out_specs=[pl.BlockSpec((bm, bn), lambda i, j: (i, j))],
        out_shape=[jax.ShapeDtypeStruct((m, n), out_dtype)],
        args=[a, b], name=name, passengers=passengers)


def _mm_nt_body(a_ref, bt_ref, o_ref):
    o_ref[...] = lax.dot_general(a_ref[...], bt_ref[...], _NT,
                                 preferred_element_type=F32).astype(o_ref.dtype)


def _mm_nt(a, bt, out_dtype, bm, bn, name, passengers=()):
    m, k = a.shape
    n = bt.shape[0]
    return _call(
        _mm_nt_body,
        grid=(m // bm, n // bn),
        in_specs=[pl.BlockSpec((bm, k), lambda i, j: (i, 0)),
                  pl.BlockSpec((bn, k), lambda i, j: (j, 0))],
        out_specs=[pl.BlockSpec((bm, bn), lambda i, j: (i, j))],
        out_shape=[jax.ShapeDtypeStruct((m, n), out_dtype)],
        args=[a, bt], name=name, passengers=passengers)


def _fb_body(a_ref, wt_ref, o_ref):
    wt = wt_ref[...].astype(BF16)
    wt = jnp.concatenate(
        [wt, jnp.zeros((LANES - GATE_RANK, wt.shape[1]), BF16)], axis=0)
    o_ref[...] = lax.dot_general(a_ref[...], wt, _NT, preferred_element_type=F32)


def _fb_proj(a, w_t, layer, bm):
    m, k = a.shape
    (fb,), _ = _call(
        _fb_body,
        grid=(m // bm,),
        in_specs=[pl.BlockSpec((bm, k), lambda i: (i, 0)),
                  pl.BlockSpec((None, GATE_RANK, k),
                               lambda i: (layer, MAIN_COLS // GATE_RANK, 0))],
        out_specs=[pl.BlockSpec((bm, LANES), lambda i: (i, 0))],
        out_shape=[jax.ShapeDtypeStruct((m, LANES), F32)],
        args=[a, w_t], name="fb_proj")
    return fb


def _mm2_body(a1_ref, a2_ref, b1_ref, b2_ref, o_ref):
    acc = jnp.dot(a1_ref[...], b1_ref[...], preferred_element_type=F32)
    acc += jnp.dot(a2_ref[...], b2_ref[...], preferred_element_type=F32)
    o_ref[...] = acc.astype(o_ref.dtype)


def _mm2(a1, a2, b, out_dtype, bm, bn, name):
    m, k1 = a1.shape
    _, k2 = a2.shape
    assert k1 == k2
    n = b.shape[1]
    (o,), _ = _call(
        _mm2_body,
        grid=(m // bm, n // bn),
        in_specs=[pl.BlockSpec((bm, k1), lambda i, j: (i, 0)),
                  pl.BlockSpec((bm, k2), lambda i, j: (i, 0)),
                  pl.BlockSpec((k1, bn), lambda i, j: (0, j)),
                  pl.BlockSpec((k2, bn), lambda i, j: (1, j))],
        out_specs=[pl.BlockSpec((bm, bn), lambda i, j: (i, j))],
        out_shape=[jax.ShapeDtypeStruct((m, n), out_dtype)],
        args=[a1, a2, b, b], name=name)
    return o


def _gateup_body(a_ref, bg_ref, bu_ref, o_ref):
    a = a_ref[...]
    g = jnp.dot(a, bg_ref[...], preferred_element_type=F32)
    u = jnp.dot(a, bu_ref[...], preferred_element_type=F32)
    o_ref[...] = (g * jax.nn.sigmoid(g) * u).astype(o_ref.dtype)


def _gateup(a, bg, bu, bm, bn, passengers=()):
    m, k = a.shape
    n = bg.shape[1]
    w_spec = pl.BlockSpec((k, bn), lambda i, j: (0, j))
    return _call(
        _gateup_body,
        grid=(m // bm, pl.cdiv(n, bn)),
        in_specs=[pl.BlockSpec((bm, k), lambda i, j: (i, 0)), w_spec, w_spec],
        out_specs=[pl.BlockSpec((bm, bn), lambda i, j: (i, j))],
        out_shape=[jax.ShapeDtypeStruct((m, n), BF16)],
        args=[a, bg, bu], name="ffn_gateup", passengers=passengers)


def _attn_body(brow_ref, q_ref, k0_ref, k1_ref, k2_ref, v0_ref, v1_ref, v2_ref,
               o_ref, bias_ref):
    qi = pl.program_id(2)

    @pl.when(qi <= ATT_NKB - 1)
    def _build_bias():
        qc = lax.broadcasted_iota(jnp.int32, (ATT_TQ, ATT_TK), 0) // CHUNK
        col = lax.broadcasted_iota(jnp.int32, (ATT_TQ, ATT_TK), 1)
        kc = col // CHUNK
        first_valid = (ATT_NKB - 1 - qi) * ATT_TQ
        keep = (kc >= qc) & (kc <= qc + LEFT_CHUNKS) & (col >= first_valid)
        for h in range(ATT_HB):
            rows = jnp.broadcast_to(brow_ref[h] * LOG2E, (ATT_TQ, BIAS_W))
            rolled = pltpu.roll(rows, 0, 1, stride=1, stride_axis=0)
            bias_ref[h] = jnp.where(keep, rolled[:, :ATT_TK], NEG_INF)

    scale = A_HEAD_DIM ** -0.5 * LOG2E
    def scores(h):
        lanes = slice(h * A_HEAD_DIM, (h + 1) * A_HEAD_DIM)
        q = (q_ref[0, :, lanes].astype(F32) * scale).astype(BF16)
        k = jnp.concatenate([k0_ref[0, :, lanes], k1_ref[0, :, lanes],
                             k2_ref[0, :, lanes]], axis=0)
        return lax.dot_general(q, k, _NT, preferred_element_type=F32) + bias_ref[h]

    s_next = scores(0)
    for h in range(ATT_HB):
        lanes = slice(h * A_HEAD_DIM, (h + 1) * A_HEAD_DIM)
        s = s_next
        if h + 1 < ATT_HB:
            s_next = scores(h + 1)
        v = jnp.concatenate([v0_ref[0, :, lanes], v1_ref[0, :, lanes],
                             v2_ref[0, :, lanes]], axis=0)
        m = jnp.max(s, axis=-1, keepdims=True)
        p = jnp.exp2(s - m)
        l = jnp.sum(p, axis=-1, keepdims=True)
        o = jnp.dot(p.astype(BF16), v, preferred_element_type=F32)
        o_ref[0, :, lanes] = (o / l).astype(o_ref.dtype)


def _band_attention(proj3, brow, passengers=()):
    b, s, _ = proj3.shape
    nq = s // ATT_TQ
    ng = A_HEADS // ATT_HB
    gq = COL_QA // ATT_W
    gk = COL_KA // ATT_W
    gv = COL_VA // ATT_W

    def kv_spec(col0, back):
        return pl.BlockSpec(
            (1, ATT_TQ, ATT_W),
            lambda bi, g, qi: (bi, jnp.maximum(qi - back, 0), col0 + g))

    return _call(
        _attn_body,
        grid=(b, ng, nq),
        in_specs=[pl.BlockSpec((ATT_HB, 1, BIAS_W), lambda bi, g, qi: (g, 0, 0)),
                  pl.BlockSpec((1, ATT_TQ, ATT_W), lambda bi, g, qi: (bi, qi, gq + g)),
                  kv_spec(gk, 2), kv_spec(gk, 1), kv_spec(gk, 0),
                  kv_spec(gv, 2), kv_spec(gv, 1), kv_spec(gv, 0)],
        out_specs=[pl.BlockSpec((1, ATT_TQ, ATT_W), lambda bi, g, qi: (bi, qi, g))],
        out_shape=[jax.ShapeDtypeStruct((b, s, A_WIDTH), BF16)],
        scratch_shapes=[pltpu.VMEM((ATT_HB, ATT_TQ, ATT_TK), F32)],
        args=[brow] + [proj3] * 7, name="band_attention", passengers=passengers)


def _bias_rows(rel_bias):
    far = rel_bias[:, 2 * REL_CLIP:]
    near = rel_bias[:, :1]
    n_far = LEFT_CHUNKS * CHUNK - REL_CLIP
    n_near = ATT_TK - n_far - (2 * REL_CLIP + 1)
    row = jnp.concatenate([
        jnp.broadcast_to(far, (A_HEADS, n_far)),
        jnp.flip(rel_bias, axis=1),
        jnp.broadcast_to(near, (A_HEADS, n_near)),
        jnp.broadcast_to(far, (A_HEADS, BIAS_W - ATT_TK)),
    ], axis=1)
    return row.reshape(A_HEADS, 1, BIAS_W)


def _split2(x):
    hi = x.astype(BF16)
    lo = (x - hi.astype(F32)).astype(BF16)
    return hi, lo


def _gla_body(q_ref, k_ref, v_ref, g_ref, fb_ref, wfg_ref, bfg_ref, nw_ref,
              o_ref, st_ref):
    step = pl.program_id(1)

    @pl.when(step == 0)
    def _reset():
        st_ref[...] = jnp.zeros_like(st_ref)

    z = jnp.dot(fb_ref[0].astype(BF16), wfg_ref[...],
                preferred_element_type=F32) + bfg_ref[...]
    log_a = (jnp.minimum(z, 0.0) - jnp.log(1.0 + jnp.exp(-jnp.abs(z)))) / GATE_TEMP
    row = lax.broadcasted_iota(jnp.int32, (GLA_ROWS, GLA_ROWS), 0)
    colm = lax.broadcasted_iota(jnp.int32, (GLA_ROWS, GLA_ROWS), 1)
    causal = (colm <= row) & (colm // CHUNK == row // CHUNK)
    tri = jnp.where(causal, 1.0, 0.0).astype(BF16)
    hi, lo = _split2(log_a)
    cum = (jnp.dot(tri, hi, preferred_element_type=F32)
           + jnp.dot(tri, lo, preferred_element_type=F32))

    q_all = q_ref[0].astype(F32) * (B_HEAD_K ** -0.5)
    k_all = k_ref[0].astype(F32)
    q_dec_all = (q_all * jnp.exp(cum)).astype(BF16)
    k_inv_all = (k_all * jnp.exp(-cum)).astype(BF16)
    k_tail, e_last = [], []
    for c in range(GLA_ROWS // CHUNK):
        rows = slice(c * CHUNK, (c + 1) * CHUNK)
        last = cum[(c + 1) * CHUNK - 1:(c + 1) * CHUNK, :]
        k_tail.append((k_all[rows] * jnp.exp(last - cum[rows])).astype(BF16))
        e_last.append(jnp.exp(last))

    for h in range(B_HEADS):
        kl = slice(h * B_HEAD_K, (h + 1) * B_HEAD_K)
        vl = slice(h * B_HEAD_V, (h + 1) * B_HEAD_V)
        q_dec = q_dec_all[:, kl]
        v = v_ref[0, :, vl]
        att = lax.dot_general(q_dec, k_inv_all[:, kl], _NT, preferred_element_type=F32)
        att = jnp.where(causal, att, 0.0).astype(BF16)
        o_intra = jnp.dot(att, v, preferred_element_type=F32)
        st = st_ref[h]
        o_parts = []
        for c in range(GLA_ROWS // CHUNK):
            rows = slice(c * CHUNK, (c + 1) * CHUNK)
            o_parts.append(o_intra[rows] + lax.dot_general(
                q_dec[rows], st.astype(BF16), _NT, preferred_element_type=F32))
            upd = lax.dot_general(v[rows], k_tail[c][:, kl], (((0,), (0,)), ((), ())),
                                  preferred_element_type=F32)
            st = st * e_last[c][:, kl] + upd
        st_ref[h] = st

        o = _rms(jnp.concatenate(o_parts, axis=0)) * nw_ref[...]
        g = g_ref[0, :, vl].astype(F32)
        o_ref[0, :, vl] = (o * (g * jax.nn.sigmoid(g))).astype(o_ref.dtype)


def _gla(proj3, fb3, wfg, bfg, nw, passengers=()):
    b, s, _ = proj3.shape
    cq = COL_QB // B_KEY_WIDTH
    ck = COL_KB // B_KEY_WIDTH
    cv = COL_VB // B_WIDTH
    cg = COL_GB // B_WIDTH
    return _call(
        _gla_body,
        grid=(b, s // GLA_ROWS),
        in_specs=[
            pl.BlockSpec((1, GLA_ROWS, B_KEY_WIDTH), lambda bi, c: (bi, c, cq)),
            pl.BlockSpec((1, GLA_ROWS, B_KEY_WIDTH), lambda bi, c: (bi, c, ck)),
            pl.BlockSpec((1, GLA_ROWS, B_WIDTH), lambda bi, c: (bi, c, cv)),
            pl.BlockSpec((1, GLA_ROWS, B_WIDTH), lambda bi, c: (bi, c, cg)),
            pl.BlockSpec((1, GLA_ROWS, LANES), lambda bi, c: (bi, c, 0)),
            pl.BlockSpec((LANES, B_KEY_WIDTH), lambda bi, c: (0, 0)),
            pl.BlockSpec((1, B_KEY_WIDTH), lambda bi, c: (0, 0)),
            pl.BlockSpec((1, B_HEAD_V), lambda bi, c: (0, 0)),
        ],
        out_specs=[pl.BlockSpec((1, GLA_ROWS, B_WIDTH), lambda bi, c: (bi, c, 0))],
        out_shape=[jax.ShapeDtypeStruct((b, s, B_WIDTH), BF16)],
        scratch_shapes=[pltpu.VMEM((B_HEADS, B_HEAD_V, B_HEAD_K), F32)],
        args=[proj3, proj3, proj3, proj3, fb3, wfg, bfg, nw], name="gla",
        passengers=passengers)


def kernel(x, norm_mix_pre, norm_mix_post, norm_ffn_pre, norm_ffn_post, w_in, rel_bias,
           w_fgate_up, b_fgate, gla_norm, w_out, w_ffn_gate, w_ffn_up, w_ffn_down):
    b, s, d = x.shape
    depth = w_in.shape[0]
    m = b * s
    hidden = w_ffn_down.shape[1]
    w_fg = jnp.pad(w_fgate_up, ((0, 0), (0, LANES - GATE_RANK), (0, 0))).astype(BF16)
    w_in_t = jnp.swapaxes(w_in, 1, 2)
    h = x.reshape(m, d)
    u = _prenorm(h, norm_mix_pre[0])
    win_b = _round_weight(Passenger(w_in_t, 0, 128, MAIN_COLS))
    for i in range(depth):
        (proj,), (wgate_b,) = _mm_nt(u, win_b, BF16, 1024, 768, "in_proj",
                                     passengers=[Passenger(w_ffn_gate, i, 128, d)])
        fb = _fb_proj(u, w_in_t, i, 1024)
        proj3 = proj.reshape(b, s, MAIN_COLS)
        (ya,), (wup_b,) = _band_attention(proj3, _bias_rows(rel_bias[i]),
                                          passengers=[Passenger(w_ffn_up, i, 64, d)])
        (yb,), (wout_b,) = _gla(proj3, fb.reshape(b, s, LANES), w_fg[i],
                                b_fgate[i].reshape(1, B_KEY_WIDTH),
                                gla_norm[i].reshape(1, B_HEAD_V),
                                passengers=[Passenger(w_out, i, 32, d)])
        mix = _mm2(ya.reshape(m, A_WIDTH), yb.reshape(m, B_WIDTH), wout_b,
                   F32, 1024, 1024, "out_proj")
        h, u = _post(h, mix, norm_mix_post[i], norm_ffn_pre[i])
        (hid,), (wdown_b,) = _gateup(
            u, wgate_b, wup_b, 2048, 256,
            passengers=[Passenger(w_ffn_down, i, hidden // 128, hidden)])
        nxt = [Passenger(w_in_t, i + 1, 128, MAIN_COLS)] if i + 1 < depth else []
        (f,), nxt_b = _mm(hid, wdown_b, F32, 512, 512, "ffn_down", passengers=nxt)
        if nxt_b:
            win_b = nxt_b[0]
        h, u = _post(h, f, norm_ffn_post[i], norm_mix_pre[i + 1] if i + 1 < depth else None)
    return h.reshape(b, s, d)
```

```python
from typing import NamedTuple

import jax
import jax.numpy as jnp
from jax import lax
from jax.experimental import pallas as pl
from jax.experimental.pallas import tpu as pltpu

D_MODEL = 4096
CHUNK = 64
LEFT_CHUNKS = 8
A_WIDTH = 2048
A_HEAD_DIM = 128
A_HEADS = A_WIDTH // A_HEAD_DIM
REL_CLIP = 128
B_WIDTH = 2048
B_HEADS = 4
B_HEAD_V = B_WIDTH // B_HEADS
B_KEY_WIDTH = B_WIDTH // 2
B_HEAD_K = B_KEY_WIDTH // B_HEADS
GATE_RANK = 16
GATE_TEMP = 16.0
EPS = 1e-6
NEG_INF = -1e30
LOG2E = 1.4426950408889634

LANES = 128
VMEM_LIMIT = 60 * 1024 * 1024

COL_QA = 0
COL_KA = A_WIDTH
COL_VA = 2 * A_WIDTH
COL_QB = 3 * A_WIDTH
COL_KB = COL_QB + B_KEY_WIDTH
COL_VB = COL_KB + B_KEY_WIDTH
COL_GB = COL_VB + B_WIDTH
MAIN_COLS = COL_GB + B_WIDTH

ATT_TQ = 256
ATT_NKB = 3
ATT_TK = ATT_NKB * ATT_TQ
ATT_HB = 8
ATT_W = ATT_HB * A_HEAD_DIM
BIAS_W = 1024
GLA_ROWS = 256

BF16 = jnp.bfloat16
F32 = jnp.float32


class Passenger(NamedTuple):
    src: jax.Array
    layer: int
    slabs: int
    rows: int


def _call(body, *, grid, in_specs, out_specs, out_shape, args, name,
          scratch_shapes=(), passengers=()):
    n_in, n_out, n_pass = len(in_specs), len(out_specs), len(passengers)
    strides = [1] * len(grid)
    for ax in range(len(grid) - 2, -1, -1):
        strides[ax] = strides[ax + 1] * grid[ax + 1]
    n_steps = strides[0] * grid[0]

    in_specs, out_specs, out_shape, args = (list(in_specs), list(out_specs),
                                            list(out_shape), list(args))
    pass_in, pass_out, pass_shape = [], [], []
    for p in passengers:
        rows, cols = p.rows, p.src.shape[2]
        assert p.slabs <= n_steps and rows % p.slabs == 0 and rows <= p.src.shape[1]
        slab = rows // p.slabs

        def slab_of(*ids, last=p.slabs - 1):
            step = sum(i * s for i, s in zip(ids, strides))
            return jnp.minimum(step, last)

        pass_in.append(pl.BlockSpec(
            (None, slab, cols),
            lambda *ids, f=slab_of, layer=p.layer: (layer, f(*ids), 0)))
        pass_out.append(pl.BlockSpec((slab, cols), lambda *ids, f=slab_of: (f(*ids), 0)))
        pass_shape.append(jax.ShapeDtypeStruct((rows, cols), BF16))

    def kern(*refs):
        ins = refs[:n_in]
        srcs = refs[n_in:n_in + n_pass]
        outs = refs[n_in + n_pass:n_in + n_pass + n_out]
        dsts = refs[n_in + n_pass + n_out:n_in + 2 * n_pass + n_out]
        scratch = refs[n_in + 2 * n_pass + n_out:]
        for s_ref, d_ref in zip(srcs, dsts):
            d_ref[...] = s_ref[...].astype(BF16)
        body(*ins, *outs, *scratch)

    res = pl.pallas_call(
        kern,
        grid=grid,
        in_specs=in_specs + pass_in,
        out_specs=out_specs + pass_out,
        out_shape=out_shape + pass_shape,
        scratch_shapes=list(scratch_shapes),
        compiler_params=pltpu.CompilerParams(
            dimension_semantics=("arbitrary",) * len(grid),
            vmem_limit_bytes=VMEM_LIMIT),
        name=name,
    )(*args, *[p.src for p in passengers])
    return res[:n_out], res[n_out:]


def _rms(x):
    return x * lax.rsqrt(jnp.mean(x * x, axis=-1, keepdims=True) + EPS)


def _prenorm_body(x_ref, w_ref, u_ref):
    u_ref[...] = (_rms(x_ref[...]) * w_ref[...]).astype(u_ref.dtype)


def _prenorm(x, w, rows=512):
    m, d = x.shape
    (u,), _ = _call(
        _prenorm_body,
        grid=(m // rows,),
        in_specs=[pl.BlockSpec((rows, d), lambda i: (i, 0)),
                  pl.BlockSpec((1, d), lambda i: (0, 0))],
        out_specs=[pl.BlockSpec((rows, d), lambda i: (i, 0))],
        out_shape=[jax.ShapeDtypeStruct((m, d), BF16)],
        args=[x, w.reshape(1, d)],
        name="prenorm")
    return u


def _post_body(h_ref, m_ref, wpost_ref, wpre_ref, hout_ref, u_ref):
    h = h_ref[...] + _rms(m_ref[...].astype(F32)) * wpost_ref[...]
    hout_ref[...] = h
    u_ref[...] = (_rms(h) * wpre_ref[...]).astype(u_ref.dtype)


def _post_last_body(h_ref, m_ref, wpost_ref, hout_ref):
    hout_ref[...] = h_ref[...] + _rms(m_ref[...].astype(F32)) * wpost_ref[...]


def _post(h, m_, wpost, wpre, rows=256):
    m, d = h.shape
    row_spec = pl.BlockSpec((rows, d), lambda i: (i, 0))
    w_spec = pl.BlockSpec((1, d), lambda i: (0, 0))
    if wpre is None:
        (h_new,), _ = _call(
            _post_last_body, grid=(m // rows,),
            in_specs=[row_spec, row_spec, w_spec], out_specs=[row_spec],
            out_shape=[jax.ShapeDtypeStruct((m, d), F32)],
            args=[h, m_, wpost.reshape(1, d)], name="post_last")
        return h_new, None
    (h_new, u), _ = _call(
        _post_body, grid=(m // rows,),
        in_specs=[row_spec, row_spec, w_spec, w_spec], out_specs=[row_spec, row_spec],
        out_shape=[jax.ShapeDtypeStruct((m, d), F32), jax.ShapeDtypeStruct((m, d), BF16)],
        args=[h, m_, wpost.reshape(1, d), wpre.reshape(1, d)], name="post")
    return h_new, u


def _round_weight(passenger):
    _, (wb,) = _call(lambda: None, grid=(passenger.slabs,), in_specs=[], out_specs=[],
                     out_shape=[], args=[], name="round_weight", passengers=[passenger])
    return wb


_NT = (((1,), (1,)), ((), ()))


def _mm_body(a_ref, b_ref, o_ref):
    o_ref[...] = jnp.dot(a_ref[...], b_ref[...],
                         preferred_element_type=F32).astype(o_ref.dtype)


def _mm(a, b, out_dtype, bm, bn, name, passengers=()):
    m, k = a.shape
    n = b.shape[1]
    return _call(
        _mm_body,
        grid=(m // bm, n // bn),
        in_specs=[pl.BlockSpec((bm, k), lambda i, j: (i, 0)),
                  pl.BlockSpec((k, bn), lambda i, j: (0, j))],
        out_specs=[pl.BlockSpec((bm, bn), lambda i, j: (i, j))],
        out_shape=[jax.ShapeDtypeStruct((m, n), out_dtype)],
        args=[a, b], name=name, passengers=passengers)


def _in_proj_body(a_ref, bt_ref, wfb_ref, o_ref, fb_ref):
    a = a_ref[...]
    o_ref[...] = lax.dot_general(a, bt_ref[...], _NT,
                                 preferred_element_type=F32).astype(o_ref.dtype)

    @pl.when(pl.program_id(1) == 0)
    def _fb():
        wt = wfb_ref[...].astype(BF16)
        wt = jnp.concatenate(
            [wt, jnp.zeros((LANES - GATE_RANK, wt.shape[1]), BF16)], axis=0)
        fb_ref[...] = lax.dot_general(a, wt, _NT, preferred_element_type=F32)


def _in_proj(a, bt, w_t, layer, bm, bn, passengers=()):
    m, k = a.shape
    n = bt.shape[0]
    return _call(
        _in_proj_body,
        grid=(m // bm, n // bn),
        in_specs=[pl.BlockSpec((bm, k), lambda i, j: (i, 0)),
                  pl.BlockSpec((bn, k), lambda i, j: (j, 0)),
                  pl.BlockSpec((None, GATE_RANK, k),
                               lambda i, j: (layer, MAIN_COLS // GATE_RANK, 0))],
        out_specs=[pl.BlockSpec((bm, bn), lambda i, j: (i, j)),
                   pl.BlockSpec((bm, LANES), lambda i, j: (i, 0))],
        out_shape=[jax.ShapeDtypeStruct((m, n), BF16),
                   jax.ShapeDtypeStruct((m, LANES), F32)],
        args=[a, bt, w_t], name="in_proj", passengers=passengers)


def _mm2_body(a1_ref, a2_ref, b1_ref, b2_ref, o_ref):
    acc = jnp.dot(a1_ref[...], b1_ref[...], preferred_element_type=F32)
    acc += jnp.dot(a2_ref[...], b2_ref[...], preferred_element_type=F32)
    o_ref[...] = acc.astype(o_ref.dtype)


def _mm2(a1, a2, b, out_dtype, bm, bn, name):
    m, k1 = a1.shape
    _, k2 = a2.shape
    assert k1 == k2
    n = b.shape[1]
    (o,), _ = _call(
        _mm2_body,
        grid=(m // bm, n // bn),
        in_specs=[pl.BlockSpec((bm, k1), lambda i, j: (i, 0)),
                  pl.BlockSpec((bm, k2), lambda i, j: (i, 0)),
                  pl.BlockSpec((k1, bn), lambda i, j: (0, j)),
                  pl.BlockSpec((k2, bn), lambda i, j: (1, j))],
        out_specs=[pl.BlockSpec((bm, bn), lambda i, j: (i, j))],
        out_shape=[jax.ShapeDtypeStruct((m, n), out_dtype)],
        args=[a1, a2, b, b], name=name)
    return o


def _gateup_body(a_ref, bg_ref, bu_ref, o_ref):
    a = a_ref[...]
    g = jnp.dot(a, bg_ref[...], preferred_element_type=F32)
    u = jnp.dot(a, bu_ref[...], preferred_element_type=F32)
    o_ref[...] = (g * jax.nn.sigmoid(g) * u).astype(o_ref.dtype)


def _gateup(a, bg, bu, bm, bn, passengers=()):
    m, k = a.shape
    n = bg.shape[1]
    w_spec = pl.BlockSpec((k, bn), lambda i, j: (0, j))
    return _call(
        _gateup_body,
        grid=(m // bm, pl.cdiv(n, bn)),
        in_specs=[pl.BlockSpec((bm, k), lambda i, j: (i, 0)), w_spec, w_spec],
        out_specs=[pl.BlockSpec((bm, bn), lambda i, j: (i, j))],
        out_shape=[jax.ShapeDtypeStruct((m, n), BF16)],
        args=[a, bg, bu], name="ffn_gateup", passengers=passengers)


def _attn_body(brow_ref, q_ref, k0_ref, k1_ref, k2_ref, v0_ref, v1_ref, v2_ref,
               o_ref, bias_ref):
    qi = pl.program_id(2)

    @pl.when(qi <= ATT_NKB - 1)
    def _build_bias():
        qc = lax.broadcasted_iota(jnp.int32, (ATT_TQ, ATT_TK), 0) // CHUNK
        col = lax.broadcasted_iota(jnp.int32, (ATT_TQ, ATT_TK), 1)
        kc = col // CHUNK
        first_valid = (ATT_NKB - 1 - qi) * ATT_TQ
        keep = (kc >= qc) & (kc <= qc + LEFT_CHUNKS) & (col >= first_valid)
        for h in range(ATT_HB):
            rows = jnp.broadcast_to(brow_ref[h] * LOG2E, (ATT_TQ, BIAS_W))
            rolled = pltpu.roll(rows, 0, 1, stride=1, stride_axis=0)
            bias_ref[h] = jnp.where(keep, rolled[:, :ATT_TK], NEG_INF)

    scale = A_HEAD_DIM ** -0.5 * LOG2E
    def scores(h):
        lanes = slice(h * A_HEAD_DIM, (h + 1) * A_HEAD_DIM)
        q = (q_ref[0, :, lanes].astype(F32) * scale).astype(BF16)
        k = jnp.concatenate([k0_ref[0, :, lanes], k1_ref[0, :, lanes],
                             k2_ref[0, :, lanes]], axis=0)
        return lax.dot_general(q, k, _NT, preferred_element_type=F32) + bias_ref[h]

    s_next = scores(0)
    for h in range(ATT_HB):
        lanes = slice(h * A_HEAD_DIM, (h + 1) * A_HEAD_DIM)
        s = s_next
        if h + 1 < ATT_HB:
            s_next = scores(h + 1)
        v = jnp.concatenate([v0_ref[0, :, lanes], v1_ref[0, :, lanes],
                             v2_ref[0, :, lanes]], axis=0)
        m = jnp.max(s, axis=-1, keepdims=True)
        p = jnp.exp2(s - m)
        l = jnp.sum(p, axis=-1, keepdims=True)
        o = jnp.dot(p.astype(BF16), v, preferred_element_type=F32)
        o_ref[0, :, lanes] = (o / l).astype(o_ref.dtype)


def _band_attention(proj3, brow, passengers=()):
    b, s, _ = proj3.shape
    nq = s // ATT_TQ
    ng = A_HEADS // ATT_HB
    gq = COL_QA // ATT_W
    gk = COL_KA // ATT_W
    gv = COL_VA // ATT_W

    def kv_spec(col0, back):
        return pl.BlockSpec(
            (1, ATT_TQ, ATT_W),
            lambda bi, g, qi: (bi, jnp.maximum(qi - back, 0), col0 + g))

    return _call(
        _attn_body,
        grid=(b, ng, nq),
        in_specs=[pl.BlockSpec((ATT_HB, 1, BIAS_W), lambda bi, g, qi: (g, 0, 0)),
                  pl.BlockSpec((1, ATT_TQ, ATT_W), lambda bi, g, qi: (bi, qi, gq + g)),
                  kv_spec(gk, 2), kv_spec(gk, 1), kv_spec(gk, 0),
                  kv_spec(gv, 2), kv_spec(gv, 1), kv_spec(gv, 0)],
        out_specs=[pl.BlockSpec((1, ATT_TQ, ATT_W), lambda bi, g, qi: (bi, qi, g))],
        out_shape=[jax.ShapeDtypeStruct((b, s, A_WIDTH), BF16)],
        scratch_shapes=[pltpu.VMEM((ATT_HB, ATT_TQ, ATT_TK), F32)],
        args=[brow] + [proj3] * 7, name="band_attention", passengers=passengers)


def _bias_rows(rel_bias):
    far = rel_bias[:, 2 * REL_CLIP:]
    near = rel_bias[:, :1]
    n_far = LEFT_CHUNKS * CHUNK - REL_CLIP
    n_near = ATT_TK - n_far - (2 * REL_CLIP + 1)
    row = jnp.concatenate([
        jnp.broadcast_to(far, (A_HEADS, n_far)),
        jnp.flip(rel_bias, axis=1),
        jnp.broadcast_to(near, (A_HEADS, n_near)),
        jnp.broadcast_to(far, (A_HEADS, BIAS_W - ATT_TK)),
    ], axis=1)
    return row.reshape(A_HEADS, 1, BIAS_W)


def _split2(x):
    hi = x.astype(BF16)
    lo = (x - hi.astype(F32)).astype(BF16)
    return hi, lo


def _gla_body(q_ref, k_ref, v_ref, g_ref, fb_ref, wfg_ref, bfg_ref, nw_ref,
              o_ref, st_ref):
    step = pl.program_id(1)

    @pl.when(step == 0)
    def _reset():
        st_ref[...] = jnp.zeros_like(st_ref)

    z = jnp.dot(fb_ref[0].astype(BF16), wfg_ref[...],
                preferred_element_type=F32) + bfg_ref[...]
    log_a = (jnp.minimum(z, 0.0) - jnp.log(1.0 + jnp.exp(-jnp.abs(z)))) / GATE_TEMP
    row = lax.broadcasted_iota(jnp.int32, (GLA_ROWS, GLA_ROWS), 0)
    colm = lax.broadcasted_iota(jnp.int32, (GLA_ROWS, GLA_ROWS), 1)
    causal = (colm <= row) & (colm // CHUNK == row // CHUNK)
    tri = jnp.where(causal, 1.0, 0.0).astype(BF16)
    hi, lo = _split2(log_a)
    cum = (jnp.dot(tri, hi, preferred_element_type=F32)
           + jnp.dot(tri, lo, preferred_element_type=F32))

    q_all = q_ref[0].astype(F32) * (B_HEAD_K ** -0.5)
    k_all = k_ref[0].astype(F32)
    q_dec_all = (q_all * jnp.exp(cum)).astype(BF16)
    k_inv_all = (k_all * jnp.exp(-cum)).astype(BF16)
    k_tail, e_last = [], []
    for c in range(GLA_ROWS // CHUNK):
        rows = slice(c * CHUNK, (c + 1) * CHUNK)
        last = cum[(c + 1) * CHUNK - 1:(c + 1) * CHUNK, :]
        k_tail.append((k_all[rows] * jnp.exp(last - cum[rows])).astype(BF16))
        e_last.append(jnp.exp(last))

    n_chunks = GLA_ROWS // CHUNK
    kls = [slice(h * B_HEAD_K, (h + 1) * B_HEAD_K) for h in range(B_HEADS)]
    vls = [slice(h * B_HEAD_V, (h + 1) * B_HEAD_V) for h in range(B_HEADS)]
    o_intra, upd = [], []
    for h in range(B_HEADS):
        v = v_ref[0, :, vls[h]]
        att = lax.dot_general(q_dec_all[:, kls[h]], k_inv_all[:, kls[h]], _NT,
                              preferred_element_type=F32)
        att = jnp.where(causal, att, 0.0).astype(BF16)
        o_intra.append(jnp.dot(att, v, preferred_element_type=F32))
        upd.append([lax.dot_general(v[c * CHUNK:(c + 1) * CHUNK],
                                    k_tail[c][:, kls[h]], (((0,), (0,)), ((), ())),
                                    preferred_element_type=F32)
                    for c in range(n_chunks)])

    st = [st_ref[h] for h in range(B_HEADS)]
    o_parts = [[] for _ in range(B_HEADS)]
    for c in range(n_chunks):
        rows = slice(c * CHUNK, (c + 1) * CHUNK)
        for h in range(B_HEADS):
            o_parts[h].append(o_intra[h][rows] + lax.dot_general(
                q_dec_all[rows, kls[h]], st[h].astype(BF16), _NT,
                preferred_element_type=F32))
            st[h] = st[h] * e_last[c][:, kls[h]] + upd[h][c]

    for h in range(B_HEADS):
        st_ref[h] = st[h]
        o = _rms(jnp.concatenate(o_parts[h], axis=0)) * nw_ref[...]
        g = g_ref[0, :, vls[h]].astype(F32)
        o_ref[0, :, vls[h]] = (o * (g * jax.nn.sigmoid(g))).astype(o_ref.dtype)


def _gla(proj3, fb3, wfg, bfg, nw, passengers=()):
    b, s, _ = proj3.shape
    cq = COL_QB // B_KEY_WIDTH
    ck = COL_KB // B_KEY_WIDTH
    cv = COL_VB // B_WIDTH
    cg = COL_GB // B_WIDTH
    return _call(
        _gla_body,
        grid=(b, s // GLA_ROWS),
        in_specs=[
            pl.BlockSpec((1, GLA_ROWS, B_KEY_WIDTH), lambda bi, c: (bi, c, cq)),
            pl.BlockSpec((1, GLA_ROWS, B_KEY_WIDTH), lambda bi, c: (bi, c, ck)),
            pl.BlockSpec((1, GLA_ROWS, B_WIDTH), lambda bi, c: (bi, c, cv)),
            pl.BlockSpec((1, GLA_ROWS, B_WIDTH), lambda bi, c: (bi, c, cg)),
            pl.BlockSpec((1, GLA_ROWS, LANES), lambda bi, c: (bi, c, 0)),
            pl.BlockSpec((LANES, B_KEY_WIDTH), lambda bi, c: (0, 0)),
            pl.BlockSpec((1, B_KEY_WIDTH), lambda bi, c: (0, 0)),
            pl.BlockSpec((1, B_HEAD_V), lambda bi, c: (0, 0)),
        ],
        out_specs=[pl.BlockSpec((1, GLA_ROWS, B_WIDTH), lambda bi, c: (bi, c, 0))],
        out_shape=[jax.ShapeDtypeStruct((b, s, B_WIDTH), BF16)],
        scratch_shapes=[pltpu.VMEM((B_HEADS, B_HEAD_V, B_HEAD_K), F32)],
        args=[proj3, proj3, proj3, proj3, fb3, wfg, bfg, nw], name="gla",
        passengers=passengers)


def kernel(x, norm_mix_pre, norm_mix_post, norm_ffn_pre, norm_ffn_post, w_in, rel_bias,
           w_fgate_up, b_fgate, gla_norm, w_out, w_ffn_gate, w_ffn_up, w_ffn_down):
    b, s, d = x.shape
    depth = w_in.shape[0]
    m = b * s
    hidden = w_ffn_down.shape[1]
    w_fg = jnp.pad(w_fgate_up, ((0, 0), (0, LANES - GATE_RANK), (0, 0))).astype(BF16)
    w_in_t = jnp.swapaxes(w_in, 1, 2)
    h = x.reshape(m, d)
    u = _prenorm(h, norm_mix_pre[0])
    win_b = _round_weight(Passenger(w_in_t, 0, 128, MAIN_COLS))
    for i in range(depth):
        (proj, fb), (wgate_b,) = _in_proj(u, win_b, w_in_t, i, 1024, 768,
                                          passengers=[Passenger(w_ffn_gate, i, 128, d)])
        proj3 = proj.reshape(b, s, MAIN_COLS)
        (ya,), (wup_b,) = _band_attention(proj3, _bias_rows(rel_bias[i]),
                                          passengers=[Passenger(w_ffn_up, i, 64, d)])
        (yb,), (wout_b,) = _gla(proj3, fb.reshape(b, s, LANES), w_fg[i],
                                b_fgate[i].reshape(1, B_KEY_WIDTH),
                                gla_norm[i].reshape(1, B_HEAD_V),
                                passengers=[Passenger(w_out, i, 32, d)])
        mix = _mm2(ya.reshape(m, A_WIDTH), yb.reshape(m, B_WIDTH), wout_b,
                   BF16, 1024, 1024, "out_proj")
        h, u = _post(h, mix, norm_mix_post[i], norm_ffn_pre[i])
        (hid,), (wdown_b,) = _gateup(
            u, wgate_b, wup_b, 2048, 256,
            passengers=[Passenger(w_ffn_down, i, hidden // 128, hidden)])
        nxt = [Passenger(w_in_t, i + 1, 128, MAIN_COLS)] if i + 1 < depth else []
        (f,), nxt_b = _mm(hid, wdown_b, BF16, 512, 512, "ffn_down", passengers=nxt)
        if nxt_b:
            win_b = nxt_b[0]
        h, u = _post(h, f, norm_ffn_post[i], norm_mix_pre[i + 1] if i + 1 < depth else None)
    return h.reshape(b, s, d)
```

```python
from typing import NamedTuple

import jax
import jax.numpy as jnp
from jax import lax
from jax.experimental import pallas as pl
from jax.experimental.pallas import tpu as pltpu

D_MODEL = 4096
CHUNK = 64
LEFT_CHUNKS = 8
A_WIDTH = 2048
A_HEAD_DIM = 128
A_HEADS = A_WIDTH // A_HEAD_DIM
REL_CLIP = 128
B_WIDTH = 2048
B_HEADS = 4
B_HEAD_V = B_WIDTH // B_HEADS
B_KEY_WIDTH = B_WIDTH // 2
B_HEAD_K = B_KEY_WIDTH // B_HEADS
GATE_RANK = 16
GATE_TEMP = 16.0
EPS = 1e-6
NEG_INF = -1e30
LOG2E = 1.4426950408889634

LANES = 128
VMEM_LIMIT = 60 * 1024 * 1024

COL_QA = 0
COL_KA = A_WIDTH
COL_VA = 2 * A_WIDTH
COL_QB = 3 * A_WIDTH
COL_KB = COL_QB + B_KEY_WIDTH
COL_VB = COL_KB + B_KEY_WIDTH
COL_GB = COL_VB + B_WIDTH
MAIN_COLS = COL_GB + B_WIDTH

ATT_TQ = 256
ATT_NKB = 3
ATT_TK = ATT_NKB * ATT_TQ
ATT_HB = 16
ATT_W = ATT_HB * A_HEAD_DIM
BIAS_W = 1024
GLA_ROWS = 256

BF16 = jnp.bfloat16
F32 = jnp.float32


class Passenger(NamedTuple):
    src: jax.Array
    layer: int
    slabs: int
    rows: int


def _call(body, *, grid, in_specs, out_specs, out_shape, args, name,
          scratch_shapes=(), passengers=()):
    n_in, n_out, n_pass = len(in_specs), len(out_specs), len(passengers)
    strides = [1] * len(grid)
    for ax in range(len(grid) - 2, -1, -1):
        strides[ax] = strides[ax + 1] * grid[ax + 1]
    n_steps = strides[0] * grid[0]

    in_specs, out_specs, out_shape, args = (list(in_specs), list(out_specs),
                                            list(out_shape), list(args))
    pass_in, pass_out, pass_shape = [], [], []
    for p in passengers:
        rows, cols = p.rows, p.src.shape[2]
        assert p.slabs <= n_steps and rows % p.slabs == 0 and rows <= p.src.shape[1]
        slab = rows // p.slabs

        def slab_of(*ids, last=p.slabs - 1):
            step = sum(i * s for i, s in zip(ids, strides))
            return jnp.minimum(step, last)

        pass_in.append(pl.BlockSpec(
            (None, slab, cols),
            lambda *ids, f=slab_of, layer=p.layer: (layer, f(*ids), 0)))
        pass_out.append(pl.BlockSpec((slab, cols), lambda *ids, f=slab_of: (f(*ids), 0)))
        pass_shape.append(jax.ShapeDtypeStruct((rows, cols), BF16))

    def kern(*refs):
        ins = refs[:n_in]
        srcs = refs[n_in:n_in + n_pass]
        outs = refs[n_in + n_pass:n_in + n_pass + n_out]
        dsts = refs[n_in + n_pass + n_out:n_in + 2 * n_pass + n_out]
        scratch = refs[n_in + 2 * n_pass + n_out:]
        for s_ref, d_ref in zip(srcs, dsts):
            d_ref[...] = s_ref[...].astype(BF16)
        body(*ins, *outs, *scratch)

    res = pl.pallas_call(
        kern,
        grid=grid,
        in_specs=in_specs + pass_in,
        out_specs=out_specs + pass_out,
        out_shape=out_shape + pass_shape,
        scratch_shapes=list(scratch_shapes),
        compiler_params=pltpu.CompilerParams(
            dimension_semantics=("arbitrary",) * len(grid),
            vmem_limit_bytes=VMEM_LIMIT),
        name=name,
    )(*args, *[p.src for p in passengers])
    return res[:n_out], res[n_out:]


def _rms(x):
    return x * lax.rsqrt(jnp.mean(x * x, axis=-1, keepdims=True) + EPS)


def _prenorm_body(x_ref, w_ref, u_ref):
    u_ref[...] = (_rms(x_ref[...]) * w_ref[...]).astype(u_ref.dtype)


def _prenorm(x, w, rows=512):
    m, d = x.shape
    (u,), _ = _call(
        _prenorm_body,
        grid=(m // rows,),
        in_specs=[pl.BlockSpec((rows, d), lambda i: (i, 0)),
                  pl.BlockSpec((1, d), lambda i: (0, 0))],
        out_specs=[pl.BlockSpec((rows, d), lambda i: (i, 0))],
        out_shape=[jax.ShapeDtypeStruct((m, d), BF16)],
        args=[x, w.reshape(1, d)],
        name="prenorm")
    return u


def _post_body(h_ref, m_ref, wpost_ref, wpre_ref, hout_ref, u_ref):
    h = h_ref[...] + _rms(m_ref[...].astype(F32)) * wpost_ref[...]
    hout_ref[...] = h
    u_ref[...] = (_rms(h) * wpre_ref[...]).astype(u_ref.dtype)


def _post_last_body(h_ref, m_ref, wpost_ref, hout_ref):
    hout_ref[...] = h_ref[...] + _rms(m_ref[...].astype(F32)) * wpost_ref[...]


def _post(h, m_, wpost, wpre, rows=256):
    m, d = h.shape
    row_spec = pl.BlockSpec((rows, d), lambda i: (i, 0))
    w_spec = pl.BlockSpec((1, d), lambda i: (0, 0))
    if wpre is None:
        (h_new,), _ = _call(
            _post_last_body, grid=(m // rows,),
            in_specs=[row_spec, row_spec, w_spec], out_specs=[row_spec],
            out_shape=[jax.ShapeDtypeStruct((m, d), F32)],
            args=[h, m_, wpost.reshape(1, d)], name="post_last")
        return h_new, None
    (h_new, u), _ = _call(
        _post_body, grid=(m // rows,),
        in_specs=[row_spec, row_spec, w_spec, w_spec], out_specs=[row_spec, row_spec],
        out_shape=[jax.ShapeDtypeStruct((m, d), F32), jax.ShapeDtypeStruct((m, d), BF16)],
        args=[h, m_, wpost.reshape(1, d), wpre.reshape(1, d)], name="post")
    return h_new, u


_NT = (((1,), (1,)), ((), ()))


def _mm_body(a_ref, b_ref, o_ref):
    o_ref[...] = jnp.dot(a_ref[...], b_ref[...],
                         preferred_element_type=F32).astype(o_ref.dtype)


def _mm(a, b, out_dtype, bm, bn, name, passengers=()):
    m, k = a.shape
    n = b.shape[1]
    return _call(
        _mm_body,
        grid=(m // bm, n // bn),
        in_specs=[pl.BlockSpec((bm, k), lambda i, j: (i, 0)),
                  pl.BlockSpec((k, bn), lambda i, j: (0, j))],
        out_specs=[pl.BlockSpec((bm, bn), lambda i, j: (i, j))],
        out_shape=[jax.ShapeDtypeStruct((m, n), out_dtype)],
        args=[a, b], name=name, passengers=passengers)


def _in_proj_body(a_ref, bt_ref, wfb_ref, o_ref, fb_ref):
    a = a_ref[...]
    o_ref[...] = lax.dot_general(a, bt_ref[...].astype(BF16), _NT,
                                 preferred_element_type=F32).astype(o_ref.dtype)

    @pl.when(pl.program_id(1) == 0)
    def _fb():
        wt = wfb_ref[...].astype(BF16)
        wt = jnp.concatenate(
            [wt, jnp.zeros((LANES - GATE_RANK, wt.shape[1]), BF16)], axis=0)
        fb_ref[...] = lax.dot_general(a, wt, _NT, preferred_element_type=F32)


def _in_proj(a, w_t, layer, bm, bn, passengers=()):
    m, k = a.shape
    return _call(
        _in_proj_body,
        grid=(m // bm, MAIN_COLS // bn),
        in_specs=[pl.BlockSpec((bm, k), lambda i, j: (i, 0)),
                  pl.BlockSpec((None, bn, k), lambda i, j: (layer, j, 0)),
                  pl.BlockSpec((None, GATE_RANK, k),
                               lambda i, j: (layer, MAIN_COLS // GATE_RANK, 0))],
        out_specs=[pl.BlockSpec((bm, bn), lambda i, j: (i, j)),
                   pl.BlockSpec((bm, LANES), lambda i, j: (i, 0))],
        out_shape=[jax.ShapeDtypeStruct((m, MAIN_COLS), BF16),
                   jax.ShapeDtypeStruct((m, LANES), F32)],
        args=[a, w_t, w_t], name="in_proj", passengers=passengers)


def _mm2_body(a1_ref, a2_ref, b1_ref, b2_ref, o_ref):
    acc = jnp.dot(a1_ref[...], b1_ref[...], preferred_element_type=F32)
    acc += jnp.dot(a2_ref[...], b2_ref[...], preferred_element_type=F32)
    o_ref[...] = acc.astype(o_ref.dtype)


def _mm2(a1, a2, b, out_dtype, bm, bn, name):
    m, k1 = a1.shape
    _, k2 = a2.shape
    assert k1 == k2
    n = b.shape[1]
    (o,), _ = _call(
        _mm2_body,
        grid=(m // bm, n // bn),
        in_specs=[pl.BlockSpec((bm, k1), lambda i, j: (i, 0)),
                  pl.BlockSpec((bm, k2), lambda i, j: (i, 0)),
                  pl.BlockSpec((k1, bn), lambda i, j: (0, j)),
                  pl.BlockSpec((k2, bn), lambda i, j: (1, j))],
        out_specs=[pl.BlockSpec((bm, bn), lambda i, j: (i, j))],
        out_shape=[jax.ShapeDtypeStruct((m, n), out_dtype)],
        args=[a1, a2, b, b], name=name)
    return o


def _gateup_body(a_ref, bg_ref, bu_ref, o_ref):
    a = a_ref[...]
    g = jnp.dot(a, bg_ref[...], preferred_element_type=F32)
    u = jnp.dot(a, bu_ref[...], preferred_element_type=F32)
    o_ref[...] = (g * jax.nn.sigmoid(g) * u).astype(o_ref.dtype)


def _gateup(a, bg, bu, bm, bn, passengers=()):
    m, k = a.shape
    n = bg.shape[1]
    w_spec = pl.BlockSpec((k, bn), lambda i, j: (0, j))
    return _call(
        _gateup_body,
        grid=(m // bm, pl.cdiv(n, bn)),
        in_specs=[pl.BlockSpec((bm, k), lambda i, j: (i, 0)), w_spec, w_spec],
        out_specs=[pl.BlockSpec((bm, bn), lambda i, j: (i, j))],
        out_shape=[jax.ShapeDtypeStruct((m, n), BF16)],
        args=[a, bg, bu], name="ffn_gateup", passengers=passengers)


def _attn_body(brow_ref, q_ref, k0_ref, k1_ref, k2_ref, v0_ref, v1_ref, v2_ref,
               o_ref, bias_ref):
    qi = pl.program_id(2)

    @pl.when(qi <= ATT_NKB - 1)
    def _build_bias():
        qc = lax.broadcasted_iota(jnp.int32, (ATT_TQ, ATT_TK), 0) // CHUNK
        col = lax.broadcasted_iota(jnp.int32, (ATT_TQ, ATT_TK), 1)
        kc = col // CHUNK
        first_valid = (ATT_NKB - 1 - qi) * ATT_TQ
        keep = (kc >= qc) & (kc <= qc + LEFT_CHUNKS) & (col >= first_valid)
        for h in range(ATT_HB):
            rows = jnp.broadcast_to(brow_ref[h] * LOG2E, (ATT_TQ, BIAS_W))
            rolled = pltpu.roll(rows, 0, 1, stride=1, stride_axis=0)
            bias_ref[h] = jnp.where(keep, rolled[:, :ATT_TK], NEG_INF)

    scale = A_HEAD_DIM ** -0.5 * LOG2E
    def scores(h):
        lanes = slice(h * A_HEAD_DIM, (h + 1) * A_HEAD_DIM)
        q = (q_ref[0, :, lanes].astype(F32) * scale).astype(BF16)
        k = jnp.concatenate([k0_ref[0, :, lanes], k1_ref[0, :, lanes],
                             k2_ref[0, :, lanes]], axis=0)
        return lax.dot_general(q, k, _NT, preferred_element_type=F32) + bias_ref[h]

    s_next = scores(0)
    for h in range(ATT_HB):
        lanes = slice(h * A_HEAD_DIM, (h + 1) * A_HEAD_DIM)
        s = s_next
        if h + 1 < ATT_HB:
            s_next = scores(h + 1)
        v = jnp.concatenate([v0_ref[0, :, lanes], v1_ref[0, :, lanes],
                             v2_ref[0, :, lanes]], axis=0)
        m = jnp.max(s, axis=-1, keepdims=True)
        p = jnp.exp2(s - m)
        l = jnp.sum(p, axis=-1, keepdims=True)
        o = jnp.dot(p.astype(BF16), v, preferred_element_type=F32)
        o_ref[0, :, lanes] = (o / l).astype(o_ref.dtype)


def _band_attention(proj3, brow, passengers=()):
    b, s, _ = proj3.shape
    nq = s // ATT_TQ
    ng = A_HEADS // ATT_HB
    gq = COL_QA // ATT_W
    gk = COL_KA // ATT_W
    gv = COL_VA // ATT_W

    def kv_spec(col0, back):
        return pl.BlockSpec(
            (1, ATT_TQ, ATT_W),
            lambda bi, g, qi: (bi, jnp.maximum(qi - back, 0), col0 + g))

    return _call(
        _attn_body,
        grid=(b, ng, nq),
        in_specs=[pl.BlockSpec((ATT_HB, 1, BIAS_W), lambda bi, g, qi: (g, 0, 0)),
                  pl.BlockSpec((1, ATT_TQ, ATT_W), lambda bi, g, qi: (bi, qi, gq + g)),
                  kv_spec(gk, 2), kv_spec(gk, 1), kv_spec(gk, 0),
                  kv_spec(gv, 2), kv_spec(gv, 1), kv_spec(gv, 0)],
        out_specs=[pl.BlockSpec((1, ATT_TQ, ATT_W), lambda bi, g, qi: (bi, qi, g))],
        out_shape=[jax.ShapeDtypeStruct((b, s, A_WIDTH), BF16)],
        scratch_shapes=[pltpu.VMEM((ATT_HB, ATT_TQ, ATT_TK), F32)],
        args=[brow] + [proj3] * 7, name="band_attention", passengers=passengers)


def _bias_rows(rel_bias):
    far = rel_bias[:, 2 * REL_CLIP:]
    near = rel_bias[:, :1]
    n_far = LEFT_CHUNKS * CHUNK - REL_CLIP
    n_near = ATT_TK - n_far - (2 * REL_CLIP + 1)
    row = jnp.concatenate([
        jnp.broadcast_to(far, (A_HEADS, n_far)),
        jnp.flip(rel_bias, axis=1),
        jnp.broadcast_to(near, (A_HEADS, n_near)),
        jnp.broadcast_to(far, (A_HEADS, BIAS_W - ATT_TK)),
    ], axis=1)
    return row.reshape(A_HEADS, 1, BIAS_W)


def _split2(x):
    hi = x.astype(BF16)
    lo = (x - hi.astype(F32)).astype(BF16)
    return hi, lo


def _gla_body(q_ref, k_ref, v_ref, g_ref, fb_ref, wfg_ref, bfg_ref, nw_ref,
              o_ref, st_ref):
    step = pl.program_id(1)

    @pl.when(step == 0)
    def _reset():
        st_ref[...] = jnp.zeros_like(st_ref)

    z = jnp.dot(fb_ref[0].astype(BF16), wfg_ref[...],
                preferred_element_type=F32) + bfg_ref[...]
    log_a = (jnp.minimum(z, 0.0) - jnp.log(1.0 + jnp.exp(-jnp.abs(z)))) / GATE_TEMP
    row = lax.broadcasted_iota(jnp.int32, (GLA_ROWS, GLA_ROWS), 0)
    colm = lax.broadcasted_iota(jnp.int32, (GLA_ROWS, GLA_ROWS), 1)
    causal = (colm <= row) & (colm // CHUNK == row // CHUNK)
    tri = jnp.where(causal, 1.0, 0.0).astype(BF16)
    hi, lo = _split2(log_a)
    cum = (jnp.dot(tri, hi, preferred_element_type=F32)
           + jnp.dot(tri, lo, preferred_element_type=F32))

    q_all = q_ref[0].astype(F32) * (B_HEAD_K ** -0.5)
    k_all = k_ref[0].astype(F32)
    q_dec_all = (q_all * jnp.exp(cum)).astype(BF16)
    k_inv_all = (k_all * jnp.exp(-cum)).astype(BF16)
    k_tail, e_last = [], []
    for c in range(GLA_ROWS // CHUNK):
        rows = slice(c * CHUNK, (c + 1) * CHUNK)
        last = cum[(c + 1) * CHUNK - 1:(c + 1) * CHUNK, :]
        k_tail.append((k_all[rows] * jnp.exp(last - cum[rows])).astype(BF16))
        e_last.append(jnp.exp(last))

    n_chunks = GLA_ROWS // CHUNK
    kls = [slice(h * B_HEAD_K, (h + 1) * B_HEAD_K) for h in range(B_HEADS)]
    vls = [slice(h * B_HEAD_V, (h + 1) * B_HEAD_V) for h in range(B_HEADS)]
    o_intra, upd = [], []
    for h in range(B_HEADS):
        v = v_ref[0, :, vls[h]]
        att = lax.dot_general(q_dec_all[:, kls[h]], k_inv_all[:, kls[h]], _NT,
                              preferred_element_type=F32)
        att = jnp.where(causal, att, 0.0).astype(BF16)
        o_intra.append(jnp.dot(att, v, preferred_element_type=F32))
        upd.append([lax.dot_general(v[c * CHUNK:(c + 1) * CHUNK],
                                    k_tail[c][:, kls[h]], (((0,), (0,)), ((), ())),
                                    preferred_element_type=F32)
                    for c in range(n_chunks)])

    st = [st_ref[h] for h in range(B_HEADS)]
    o_parts = [[] for _ in range(B_HEADS)]
    for c in range(n_chunks):
        rows = slice(c * CHUNK, (c + 1) * CHUNK)
        for h in range(B_HEADS):
            o_parts[h].append(o_intra[h][rows] + lax.dot_general(
                q_dec_all[rows, kls[h]], st[h].astype(BF16), _NT,
                preferred_element_type=F32))
            st[h] = st[h] * e_last[c][:, kls[h]] + upd[h][c]

    for h in range(B_HEADS):
        st_ref[h] = st[h]
        o = _rms(jnp.concatenate(o_parts[h], axis=0)) * nw_ref[...]
        g = g_ref[0, :, vls[h]].astype(F32)
        o_ref[0, :, vls[h]] = (o * (g * jax.nn.sigmoid(g))).astype(o_ref.dtype)


def _gla(proj3, fb3, wfg, bfg, nw, passengers=()):
    b, s, _ = proj3.shape
    cq = COL_QB // B_KEY_WIDTH
    ck = COL_KB // B_KEY_WIDTH
    cv = COL_VB // B_WIDTH
    cg = COL_GB // B_WIDTH
    return _call(
        _gla_body,
        grid=(b, s // GLA_ROWS),
        in_specs=[
            pl.BlockSpec((1, GLA_ROWS, B_KEY_WIDTH), lambda bi, c: (bi, c, cq)),
            pl.BlockSpec((1, GLA_ROWS, B_KEY_WIDTH), lambda bi, c: (bi, c, ck)),
            pl.BlockSpec((1, GLA_ROWS, B_WIDTH), lambda bi, c: (bi, c, cv)),
            pl.BlockSpec((1, GLA_ROWS, B_WIDTH), lambda bi, c: (bi, c, cg)),
            pl.BlockSpec((1, GLA_ROWS, LANES), lambda bi, c: (bi, c, 0)),
            pl.BlockSpec((LANES, B_KEY_WIDTH), lambda bi, c: (0, 0)),
            pl.BlockSpec((1, B_KEY_WIDTH), lambda bi, c: (0, 0)),
            pl.BlockSpec((1, B_HEAD_V), lambda bi, c: (0, 0)),
        ],
        out_specs=[pl.BlockSpec((1, GLA_ROWS, B_WIDTH), lambda bi, c: (bi, c, 0))],
        out_shape=[jax.ShapeDtypeStruct((b, s, B_WIDTH), BF16)],
        scratch_shapes=[pltpu.VMEM((B_HEADS, B_HEAD_V, B_HEAD_K), F32)],
        args=[proj3, proj3, proj3, proj3, fb3, wfg, bfg, nw], name="gla",
        passengers=passengers)


def kernel(x, norm_mix_pre, norm_mix_post, norm_ffn_pre, norm_ffn_post, w_in, rel_bias,
           w_fgate_up, b_fgate, gla_norm, w_out, w_ffn_gate, w_ffn_up, w_ffn_down):
    b, s, d = x.shape
    depth = w_in.shape[0]
    m = b * s
    hidden = w_ffn_down.shape[1]
    w_fg = jnp.pad(w_fgate_up, ((0, 0), (0, LANES - GATE_RANK), (0, 0))).astype(BF16)
    w_in_t = jnp.swapaxes(w_in, 1, 2)
    h = x.reshape(m, d)
    u = _prenorm(h, norm_mix_pre[0])
    for i in range(depth):
        (proj, fb), (wgate_b,) = _in_proj(u, w_in_t, i, 1024, 768,
                                          passengers=[Passenger(w_ffn_gate, i, 128, d)])
        proj3 = proj.reshape(b, s, MAIN_COLS)
        (ya,), (wup_b,) = _band_attention(proj3, _bias_rows(rel_bias[i]),
                                          passengers=[Passenger(w_ffn_up, i, 32, d)])
        (yb,), (wout_b,) = _gla(proj3, fb.reshape(b, s, LANES), w_fg[i],
                                b_fgate[i].reshape(1, B_KEY_WIDTH),
                                gla_norm[i].reshape(1, B_HEAD_V),
                                passengers=[Passenger(w_out, i, 32, d)])
        mix = _mm2(ya.reshape(m, A_WIDTH), yb.reshape(m, B_WIDTH), wout_b,
                   BF16, 1024, 1024, "out_proj")
        h, u = _post(h, mix, norm_mix_post[i], norm_ffn_pre[i])
        (hid,), (wdown_b,) = _gateup(
            u, wgate_b, wup_b, 2048, 256,
            passengers=[Passenger(w_ffn_down, i, hidden // 128, hidden)])
        (f,), _ = _mm(hid, wdown_b, BF16, 512, 512, "ffn_down")
        h, u = _post(h, f, norm_ffn_post[i], norm_mix_pre[i + 1] if i + 1 < depth else None)
    return h.reshape(b, s, d)
```

```python
from typing import NamedTuple

import jax
import jax.numpy as jnp
from jax import lax
from jax.experimental import pallas as pl
from jax.experimental.pallas import tpu as pltpu

CHUNK = 64
LEFT_CHUNKS = 8
A_WIDTH = 2048
A_HEAD_DIM = 128
A_HEADS = A_WIDTH // A_HEAD_DIM
REL_CLIP = 128
B_WIDTH = 2048
B_HEADS = 4
B_HEAD_V = B_WIDTH // B_HEADS
B_KEY_WIDTH = B_WIDTH // 2
B_HEAD_K = B_KEY_WIDTH // B_HEADS
GATE_RANK = 16
GATE_TEMP = 16.0
EPS = 1e-6
NEG_INF = -1e30
LOG2E = 1.4426950408889634

LANES = 128
BF16_SUBLANES = 16
VMEM_LIMIT = 60 * 1024 * 1024

IN_PROJ_BLOCK = (1024, 768)
OUT_PROJ_BLOCK = (1024, 1024)
GATEUP_BLOCK = (2048, 256)
DOWN_BLOCK = (512, 512)
GATEUP_ROW_SPLIT = 2

COL_QA = 0
COL_KA = A_WIDTH
COL_VA = 2 * A_WIDTH
COL_QB = 3 * A_WIDTH
COL_KB = COL_QB + B_KEY_WIDTH
COL_VB = COL_KB + B_KEY_WIDTH
COL_GB = COL_VB + B_WIDTH
MAIN_COLS = COL_GB + B_WIDTH

ATT_TQ = 4 * CHUNK
ATT_NKB = LEFT_CHUNKS * CHUNK // ATT_TQ + 1
ATT_TK = ATT_NKB * ATT_TQ
ATT_HB = A_HEADS
ATT_W = ATT_HB * A_HEAD_DIM
BIAS_W = ATT_TK + ATT_TQ
ATT_SKIP = ATT_TQ // 2
GLA_ROWS = 4 * CHUNK

BF16 = jnp.bfloat16
F32 = jnp.float32


class Passenger(NamedTuple):
    src: jax.Array
    layer: int


def _slab_count(rows, n_steps):
    tiles = rows // BF16_SUBLANES
    assert tiles * BF16_SUBLANES == rows
    return max(s for s in range(1, min(tiles, n_steps) + 1) if tiles % s == 0)


def _call(body, *, grid, in_specs, out_specs, out_shape, args, name,
          scratch_shapes=(), passengers=()):
    n_in, n_out, n_pass = len(in_specs), len(out_specs), len(passengers)
    strides = [1] * len(grid)
    for ax in range(len(grid) - 2, -1, -1):
        strides[ax] = strides[ax + 1] * grid[ax + 1]
    n_steps = strides[0] * grid[0]

    in_specs, out_specs, out_shape, args = (list(in_specs), list(out_specs),
                                            list(out_shape), list(args))
    pass_in, pass_out, pass_shape = [], [], []
    for p in passengers:
        _, rows, cols = p.src.shape
        slabs = _slab_count(rows, n_steps)
        slab = rows // slabs

        def slab_of(*ids, last=slabs - 1):
            step = sum(i * s for i, s in zip(ids, strides))
            return jnp.minimum(step, last)

        pass_in.append(pl.BlockSpec(
            (None, slab, cols),
            lambda *ids, f=slab_of, layer=p.layer: (layer, f(*ids), 0)))
        pass_out.append(pl.BlockSpec((slab, cols), lambda *ids, f=slab_of: (f(*ids), 0)))
        pass_shape.append(jax.ShapeDtypeStruct((rows, cols), BF16))

    def kern(*refs):
        ins = refs[:n_in]
        srcs = refs[n_in:n_in + n_pass]
        outs = refs[n_in + n_pass:n_in + n_pass + n_out]
        dsts = refs[n_in + n_pass + n_out:n_in + 2 * n_pass + n_out]
        scratch = refs[n_in + 2 * n_pass + n_out:]
        for s_ref, d_ref in zip(srcs, dsts):
            d_ref[...] = s_ref[...].astype(BF16)
        body(*ins, *outs, *scratch)

    res = pl.pallas_call(
        kern,
        grid=grid,
        in_specs=in_specs + pass_in,
        out_specs=out_specs + pass_out,
        out_shape=out_shape + pass_shape,
        scratch_shapes=list(scratch_shapes),
        compiler_params=pltpu.CompilerParams(
            dimension_semantics=("arbitrary",) * len(grid),
            vmem_limit_bytes=VMEM_LIMIT),
        name=name,
    )(*args, *[p.src for p in passengers])
    return res[:n_out], res[n_out:]


def _rms(x):
    return x * lax.rsqrt(jnp.mean(x * x, axis=-1, keepdims=True) + EPS)


def _prenorm_body(x_ref, w_ref, u_ref):
    u_ref[...] = (_rms(x_ref[...]) * w_ref[...]).astype(u_ref.dtype)


def _prenorm(x, w, rows=512):
    m, d = x.shape
    (u,), _ = _call(
        _prenorm_body,
        grid=(m // rows,),
        in_specs=[pl.BlockSpec((rows, d), lambda i: (i, 0)),
                  pl.BlockSpec((1, d), lambda i: (0, 0))],
        out_specs=[pl.BlockSpec((rows, d), lambda i: (i, 0))],
        out_shape=[jax.ShapeDtypeStruct((m, d), BF16)],
        args=[x, w.reshape(1, d)],
        name="prenorm")
    return u


def _post_body(h_ref, m_ref, wpost_ref, wpre_ref, hout_ref, u_ref):
    h = h_ref[...] + _rms(m_ref[...].astype(F32)) * wpost_ref[...]
    hout_ref[...] = h
    u_ref[...] = (_rms(h) * wpre_ref[...]).astype(u_ref.dtype)


def _post_last_body(h_ref, m_ref, wpost_ref, hout_ref):
    hout_ref[...] = h_ref[...] + _rms(m_ref[...].astype(F32)) * wpost_ref[...]


def _post(h, m_, wpost, wpre, rows=256):
    m, d = h.shape
    row_spec = pl.BlockSpec((rows, d), lambda i: (i, 0))
    w_spec = pl.BlockSpec((1, d), lambda i: (0, 0))
    if wpre is None:
        (h_new,), _ = _call(
            _post_last_body, grid=(m // rows,),
            in_specs=[row_spec, row_spec, w_spec], out_specs=[row_spec],
            out_shape=[jax.ShapeDtypeStruct((m, d), F32)],
            args=[h, m_, wpost.reshape(1, d)], name="post_last")
        return h_new, None
    (h_new, u), _ = _call(
        _post_body, grid=(m // rows,),
        in_specs=[row_spec, row_spec, w_spec, w_spec], out_specs=[row_spec, row_spec],
        out_shape=[jax.ShapeDtypeStruct((m, d), F32), jax.ShapeDtypeStruct((m, d), BF16)],
        args=[h, m_, wpost.reshape(1, d), wpre.reshape(1, d)], name="post")
    return h_new, u


_NT = (((1,), (1,)), ((), ()))


def _block_grid(m, n, bm, bn):
    assert m % bm == 0 and n % bn == 0, (m, n, bm, bn)
    return (m // bm, n // bn)


def _mm_body(a_ref, b_ref, o_ref):
    o_ref[...] = jnp.dot(a_ref[...], b_ref[...],
                         preferred_element_type=F32).astype(o_ref.dtype)


def _mm(a, b, out_dtype, bm, bn, name, passengers=()):
    m, k = a.shape
    n = b.shape[1]
    return _call(
        _mm_body,
        grid=_block_grid(m, n, bm, bn),
        in_specs=[pl.BlockSpec((bm, k), lambda i, j: (i, 0)),
                  pl.BlockSpec((k, bn), lambda i, j: (0, j))],
        out_specs=[pl.BlockSpec((bm, bn), lambda i, j: (i, j))],
        out_shape=[jax.ShapeDtypeStruct((m, n), out_dtype)],
        args=[a, b], name=name, passengers=passengers)


def _in_proj_body(a_ref, bt_ref, wfb_ref, o_ref, fb_ref):
    a = a_ref[...]
    o_ref[...] = lax.dot_general(a, bt_ref[...].astype(BF16), _NT,
                                 preferred_element_type=F32).astype(o_ref.dtype)

    @pl.when(pl.program_id(1) == 0)
    def _fb():
        wt = wfb_ref[...].astype(BF16)
        wt = jnp.concatenate(
            [wt, jnp.zeros((LANES - GATE_RANK, wt.shape[1]), BF16)], axis=0)
        fb_ref[...] = lax.dot_general(a, wt, _NT, preferred_element_type=F32)


def _in_proj(a, w_t, layer, bm, bn, passengers=()):
    m, k = a.shape
    return _call(
        _in_proj_body,
        grid=_block_grid(m, MAIN_COLS, bm, bn),
        in_specs=[pl.BlockSpec((bm, k), lambda i, j: (i, 0)),
                  pl.BlockSpec((None, bn, k), lambda i, j: (layer, j, 0)),
                  pl.BlockSpec((None, GATE_RANK, k),
                               lambda i, j: (layer, MAIN_COLS // GATE_RANK, 0))],
        out_specs=[pl.BlockSpec((bm, bn), lambda i, j: (i, j)),
                   pl.BlockSpec((bm, LANES), lambda i, j: (i, 0))],
        out_shape=[jax.ShapeDtypeStruct((m, MAIN_COLS), BF16),
                   jax.ShapeDtypeStruct((m, LANES), F32)],
        args=[a, w_t, w_t], name="in_proj", passengers=passengers)


def _mm2_body(a1_ref, a2_ref, b1_ref, b2_ref, o_ref):
    acc = jnp.dot(a1_ref[...], b1_ref[...], preferred_element_type=F32)
    acc += jnp.dot(a2_ref[...], b2_ref[...], preferred_element_type=F32)
    o_ref[...] = acc.astype(o_ref.dtype)


def _mm2(a1, a2, b, out_dtype, bm, bn, name):
    m, k1 = a1.shape
    _, k2 = a2.shape
    assert k1 == k2
    n = b.shape[1]
    (o,), _ = _call(
        _mm2_body,
        grid=_block_grid(m, n, bm, bn),
        in_specs=[pl.BlockSpec((bm, k1), lambda i, j: (i, 0)),
                  pl.BlockSpec((bm, k2), lambda i, j: (i, 0)),
                  pl.BlockSpec((k1, bn), lambda i, j: (0, j)),
                  pl.BlockSpec((k2, bn), lambda i, j: (1, j))],
        out_specs=[pl.BlockSpec((bm, bn), lambda i, j: (i, j))],
        out_shape=[jax.ShapeDtypeStruct((m, n), out_dtype)],
        args=[a1, a2, b, b], name=name)
    return o


def _gateup_body(a_ref, bg_ref, bu_ref, o_ref):
    bg = bg_ref[...]
    bu = bu_ref[...]
    rows_per = a_ref.shape[0] // GATEUP_ROW_SPLIT
    for r in range(GATEUP_ROW_SPLIT):
        rows = slice(r * rows_per, (r + 1) * rows_per)
        a = a_ref[rows, :]
        g = jnp.dot(a, bg, preferred_element_type=F32)
        u = jnp.dot(a, bu, preferred_element_type=F32)
        o_ref[rows, :] = (g * jax.nn.sigmoid(g) * u).astype(o_ref.dtype)


def _gateup(a, bg, bu, bm, bn, passengers=()):
    m, k = a.shape
    n = bg.shape[1]
    w_spec = pl.BlockSpec((k, bn), lambda i, j: (0, j))
    return _call(
        _gateup_body,
        grid=_block_grid(m, n, bm, bn),
        in_specs=[pl.BlockSpec((bm, k), lambda i, j: (i, 0)), w_spec, w_spec],
        out_specs=[pl.BlockSpec((bm, bn), lambda i, j: (i, j))],
        out_shape=[jax.ShapeDtypeStruct((m, n), BF16)],
        args=[a, bg, bu], name="ffn_gateup", passengers=passengers)


def _attn_body(brow_ref, q_ref, k0_ref, k1_ref, k2_ref, v0_ref, v1_ref, v2_ref,
               o_ref, bias_ref):
    qi = pl.program_id(2)

    @pl.when(qi <= ATT_NKB - 1)
    def _build_bias():
        qc = lax.broadcasted_iota(jnp.int32, (ATT_TQ, ATT_TK), 0) // CHUNK
        col = lax.broadcasted_iota(jnp.int32, (ATT_TQ, ATT_TK), 1)
        kc = col // CHUNK
        first_valid = (ATT_NKB - 1 - qi) * ATT_TQ
        keep = (kc >= qc) & (kc <= qc + LEFT_CHUNKS) & (col >= first_valid)
        for h in range(ATT_HB):
            rows = jnp.broadcast_to(brow_ref[h] * LOG2E, (ATT_TQ, BIAS_W))
            rolled = pltpu.roll(rows, 0, 1, stride=1, stride_axis=0)
            bias_ref[h] = jnp.where(keep, rolled[:, :ATT_TK], NEG_INF)

    scale = A_HEAD_DIM ** -0.5 * LOG2E
    def scores(h):
        lanes = slice(h * A_HEAD_DIM, (h + 1) * A_HEAD_DIM)
        q = (q_ref[0, :, lanes].astype(F32) * scale).astype(BF16)
        k = jnp.concatenate([k0_ref[0, :, lanes], k1_ref[0, :, lanes],
                             k2_ref[0, :, lanes]], axis=0)
        return lax.dot_general(q, k, _NT, preferred_element_type=F32) + bias_ref[h]

    lane = lax.broadcasted_iota(jnp.int32, (ATT_TK, A_HEAD_DIM), 1)
    ones_col = jnp.where(lane == 0, 1.0, 0.0).astype(BF16)

    s_next = scores(0)
    for h in range(ATT_HB):
        lanes = slice(h * A_HEAD_DIM, (h + 1) * A_HEAD_DIM)
        s = s_next
        if h + 1 < ATT_HB:
            s_next = scores(h + 1)
        v = jnp.concatenate([v0_ref[0, :, lanes], v1_ref[0, :, lanes],
                             v2_ref[0, :, lanes]], axis=0)
        s_top = s[:ATT_TQ // 2, :ATT_TK - ATT_SKIP]
        s_bot = s[ATT_TQ // 2:, ATT_SKIP:]
        p_top = jnp.exp2(s_top - jnp.max(s_top, axis=-1, keepdims=True)).astype(BF16)
        p_bot = jnp.exp2(s_bot - jnp.max(s_bot, axis=-1, keepdims=True)).astype(BF16)
        zeros = jnp.zeros((ATT_TQ // 2, ATT_SKIP), BF16)
        p = jnp.concatenate([jnp.concatenate([p_top, zeros], axis=1),
                             jnp.concatenate([zeros, p_bot], axis=1)], axis=0)
        ol = jnp.dot(p, jnp.concatenate([v, ones_col], axis=1),
                     preferred_element_type=F32)
        o = ol[:, :A_HEAD_DIM]
        l = ol[:, A_HEAD_DIM:A_HEAD_DIM + 1]
        o_ref[0, :, lanes] = (o / l).astype(o_ref.dtype)


def _band_attention(proj3, brow, passengers=()):
    b, s, _ = proj3.shape
    nq = s // ATT_TQ
    ng = A_HEADS // ATT_HB
    gq = COL_QA // ATT_W
    gk = COL_KA // ATT_W
    gv = COL_VA // ATT_W

    def kv_spec(col0, back):
        return pl.BlockSpec(
            (1, ATT_TQ, ATT_W),
            lambda bi, g, qi: (bi, jnp.maximum(qi - back, 0), col0 + g))

    return _call(
        _attn_body,
        grid=(b, ng, nq),
        in_specs=[pl.BlockSpec((ATT_HB, 1, BIAS_W), lambda bi, g, qi: (g, 0, 0)),
                  pl.BlockSpec((1, ATT_TQ, ATT_W), lambda bi, g, qi: (bi, qi, gq + g)),
                  kv_spec(gk, 2), kv_spec(gk, 1), kv_spec(gk, 0),
                  kv_spec(gv, 2), kv_spec(gv, 1), kv_spec(gv, 0)],
        out_specs=[pl.BlockSpec((1, ATT_TQ, ATT_W), lambda bi, g, qi: (bi, qi, g))],
        out_shape=[jax.ShapeDtypeStruct((b, s, A_WIDTH), BF16)],
        scratch_shapes=[pltpu.VMEM((ATT_HB, ATT_TQ, ATT_TK), F32)],
        args=[brow] + [proj3] * 7, name="band_attention", passengers=passengers)


def _bias_rows(rel_bias):
    far = rel_bias[:, 2 * REL_CLIP:]
    near = rel_bias[:, :1]
    n_far = LEFT_CHUNKS * CHUNK - REL_CLIP
    n_near = ATT_TK - n_far - (2 * REL_CLIP + 1)
    row = jnp.concatenate([
        jnp.broadcast_to(far, (A_HEADS, n_far)),
        jnp.flip(rel_bias, axis=1),
        jnp.broadcast_to(near, (A_HEADS, n_near)),
        jnp.broadcast_to(far, (A_HEADS, BIAS_W - ATT_TK)),
    ], axis=1)
    return row.reshape(A_HEADS, 1, BIAS_W)


def _split2(x):
    hi = x.astype(BF16)
    lo = (x - hi.astype(F32)).astype(BF16)
    return hi, lo


def _gla_body(q_ref, k_ref, v_ref, g_ref, fb_ref, wfg_ref, bfg_ref, nw_ref,
              o_ref, st_ref):
    step = pl.program_id(1)

    @pl.when(step == 0)
    def _reset():
        st_ref[...] = jnp.zeros_like(st_ref)

    z = jnp.dot(fb_ref[0].astype(BF16), wfg_ref[...],
                preferred_element_type=F32) + bfg_ref[...]
    log_a = (jnp.minimum(z, 0.0) - jnp.log(1.0 + jnp.exp(-jnp.abs(z)))) / GATE_TEMP
    row = lax.broadcasted_iota(jnp.int32, (GLA_ROWS, GLA_ROWS), 0)
    colm = lax.broadcasted_iota(jnp.int32, (GLA_ROWS, GLA_ROWS), 1)
    causal = (colm <= row) & (colm // CHUNK == row // CHUNK)
    tri = jnp.where(causal, 1.0, 0.0).astype(BF16)
    hi, lo = _split2(log_a)
    cum = (jnp.dot(tri, hi, preferred_element_type=F32)
           + jnp.dot(tri, lo, preferred_element_type=F32))

    q_all = q_ref[0].astype(F32) * (B_HEAD_K ** -0.5)
    k_all = k_ref[0].astype(F32)
    q_dec_all = (q_all * jnp.exp(cum)).astype(BF16)
    k_inv_all = (k_all * jnp.exp(-cum)).astype(BF16)
    k_tail, e_last = [], []
    for c in range(GLA_ROWS // CHUNK):
        rows = slice(c * CHUNK, (c + 1) * CHUNK)
        last = cum[(c + 1) * CHUNK - 1:(c + 1) * CHUNK, :]
        k_tail.append((k_all[rows] * jnp.exp(last - cum[rows])).astype(BF16))
        e_last.append(jnp.exp(last))

    n_chunks = GLA_ROWS // CHUNK
    kls = [slice(h * B_HEAD_K, (h + 1) * B_HEAD_K) for h in range(B_HEADS)]
    vls = [slice(h * B_HEAD_V, (h + 1) * B_HEAD_V) for h in range(B_HEADS)]
    o_intra, upd = [], []
    for h in range(B_HEADS):
        v = v_ref[0, :, vls[h]]
        att = lax.dot_general(q_dec_all[:, kls[h]], k_inv_all[:, kls[h]], _NT,
                              preferred_element_type=F32)
        att = jnp.where(causal, att, 0.0).astype(BF16)
        o_intra.append(jnp.dot(att, v, preferred_element_type=F32))
        upd.append([lax.dot_general(v[c * CHUNK:(c + 1) * CHUNK],
                                    k_tail[c][:, kls[h]], (((0,), (0,)), ((), ())),
                                    preferred_element_type=F32)
                    for c in range(n_chunks)])

    st = [st_ref[h] for h in range(B_HEADS)]
    o_parts = [[] for _ in range(B_HEADS)]
    for c in range(n_chunks):
        rows = slice(c * CHUNK, (c + 1) * CHUNK)
        for h in range(B_HEADS):
            o_parts[h].append(o_intra[h][rows] + lax.dot_general(
                q_dec_all[rows, kls[h]], st[h].astype(BF16), _NT,
                preferred_element_type=F32))
            st[h] = st[h] * e_last[c][:, kls[h]] + upd[h][c]

    for h in range(B_HEADS):
        st_ref[h] = st[h]
        o = _rms(jnp.concatenate(o_parts[h], axis=0)) * nw_ref[...]
        g = g_ref[0, :, vls[h]].astype(F32)
        o_ref[0, :, vls[h]] = (o * (g * jax.nn.sigmoid(g))).astype(o_ref.dtype)


def _gla(proj3, fb3, wfg, bfg, nw, passengers=()):
    b, s, _ = proj3.shape
    cq = COL_QB // B_KEY_WIDTH
    ck = COL_KB // B_KEY_WIDTH
    cv = COL_VB // B_WIDTH
    cg = COL_GB // B_WIDTH
    return _call(
        _gla_body,
        grid=(b, s // GLA_ROWS),
        in_specs=[
            pl.BlockSpec((1, GLA_ROWS, B_KEY_WIDTH), lambda bi, c: (bi, c, cq)),
            pl.BlockSpec((1, GLA_ROWS, B_KEY_WIDTH), lambda bi, c: (bi, c, ck)),
            pl.BlockSpec((1, GLA_ROWS, B_WIDTH), lambda bi, c: (bi, c, cv)),
            pl.BlockSpec((1, GLA_ROWS, B_WIDTH), lambda bi, c: (bi, c, cg)),
            pl.BlockSpec((1, GLA_ROWS, LANES), lambda bi, c: (bi, c, 0)),
            pl.BlockSpec((LANES, B_KEY_WIDTH), lambda bi, c: (0, 0)),
            pl.BlockSpec((1, B_KEY_WIDTH), lambda bi, c: (0, 0)),
            pl.BlockSpec((1, B_HEAD_V), lambda bi, c: (0, 0)),
        ],
        out_specs=[pl.BlockSpec((1, GLA_ROWS, B_WIDTH), lambda bi, c: (bi, c, 0))],
        out_shape=[jax.ShapeDtypeStruct((b, s, B_WIDTH), BF16)],
        scratch_shapes=[pltpu.VMEM((B_HEADS, B_HEAD_V, B_HEAD_K), F32)],
        args=[proj3, proj3, proj3, proj3, fb3, wfg, bfg, nw], name="gla",
        passengers=passengers)


def kernel(x, norm_mix_pre, norm_mix_post, norm_ffn_pre, norm_ffn_post, w_in, rel_bias,
           w_fgate_up, b_fgate, gla_norm, w_out, w_ffn_gate, w_ffn_up, w_ffn_down):
    b, s, d = x.shape
    depth = w_in.shape[0]
    m = b * s
    w_fg = jnp.pad(w_fgate_up, ((0, 0), (0, LANES - GATE_RANK), (0, 0))).astype(BF16)
    w_in_t = jnp.swapaxes(w_in, 1, 2)
    h = x.reshape(m, d)
    u = _prenorm(h, norm_mix_pre[0])
    for i in range(depth):
        (proj, fb), (wgate_b,) = _in_proj(u, w_in_t, i, *IN_PROJ_BLOCK,
                                          passengers=[Passenger(w_ffn_gate, i)])
        proj3 = proj.reshape(b, s, MAIN_COLS)
        (ya,), (wup_b,) = _band_attention(proj3, _bias_rows(rel_bias[i]),
                                          passengers=[Passenger(w_ffn_up, i)])
        (yb,), (wout_b,) = _gla(proj3, fb.reshape(b, s, LANES), w_fg[i],
                                b_fgate[i].reshape(1, B_KEY_WIDTH),
                                gla_norm[i].reshape(1, B_HEAD_V),
                                passengers=[Passenger(w_out, i)])
        mix = _mm2(ya.reshape(m, A_WIDTH), yb.reshape(m, B_WIDTH), wout_b,
                   BF16, *OUT_PROJ_BLOCK, "out_proj")
        h, u = _post(h, mix, norm_mix_post[i], norm_ffn_pre[i])
        (hid,), (wdown_b,) = _gateup(u, wgate_b, wup_b, *GATEUP_BLOCK,
                                     passengers=[Passenger(w_ffn_down, i)])
        (f,), _ = _mm(hid, wdown_b, BF16, *DOWN_BLOCK, "ffn_down")
        h, u = _post(h, f, norm_ffn_post[i], norm_mix_pre[i + 1] if i + 1 < depth else None)
    return h.reshape(b, s, d)
```

```python
from typing import NamedTuple

import jax
import jax.numpy as jnp
from jax import lax
from jax.experimental import pallas as pl
from jax.experimental.pallas import tpu as pltpu

CHUNK = 64
LEFT_CHUNKS = 8
A_WIDTH = 2048
A_HEAD_DIM = 128
A_HEADS = A_WIDTH // A_HEAD_DIM
REL_CLIP = 128
B_WIDTH = 2048
B_HEADS = 4
B_HEAD_V = B_WIDTH // B_HEADS
B_KEY_WIDTH = B_WIDTH // 2
B_HEAD_K = B_KEY_WIDTH // B_HEADS
GATE_RANK = 16
GATE_TEMP = 16.0
EPS = 1e-6
NEG_INF = -1e30
LOG2E = 1.4426950408889634

LANES = 128
BF16_SUBLANES = 16
VMEM_LIMIT = 60 * 1024 * 1024

IN_PROJ_BLOCK = (1024, 768)
OUT_PROJ_BLOCK = (1024, 1024)
GATEUP_BLOCK = (2048, 256)
DOWN_BLOCK = (512, 512)
GATEUP_ROW_SPLIT = 2

COL_QA = 0
COL_KA = A_WIDTH
COL_VA = 2 * A_WIDTH
COL_QB = 3 * A_WIDTH
COL_KB = COL_QB + B_KEY_WIDTH
COL_VB = COL_KB + B_KEY_WIDTH
COL_GB = COL_VB + B_WIDTH
MAIN_COLS = COL_GB + B_WIDTH

ATT_TQ = 4 * CHUNK
ATT_NKB = LEFT_CHUNKS * CHUNK // ATT_TQ + 1
ATT_TK = ATT_NKB * ATT_TQ
ATT_HB = A_HEADS
ATT_W = ATT_HB * A_HEAD_DIM
BIAS_W = ATT_TK + ATT_TQ
ATT_SKIP = ATT_TQ // 2
GLA_ROWS = 4 * CHUNK

BF16 = jnp.bfloat16
F32 = jnp.float32


class Passenger(NamedTuple):
    src: jax.Array
    layer: int


def _slab_count(rows, n_steps):
    tiles = rows // BF16_SUBLANES
    assert tiles * BF16_SUBLANES == rows
    return max(s for s in range(1, min(tiles, n_steps) + 1) if tiles % s == 0)


def _call(body, *, grid, in_specs, out_specs, out_shape, args, name,
          scratch_shapes=(), passengers=()):
    n_in, n_out, n_pass = len(in_specs), len(out_specs), len(passengers)
    strides = [1] * len(grid)
    for ax in range(len(grid) - 2, -1, -1):
        strides[ax] = strides[ax + 1] * grid[ax + 1]
    n_steps = strides[0] * grid[0]

    in_specs, out_specs, out_shape, args = (list(in_specs), list(out_specs),
                                            list(out_shape), list(args))
    pass_in, pass_out, pass_shape = [], [], []
    for p in passengers:
        _, rows, cols = p.src.shape
        slabs = _slab_count(rows, n_steps)
        slab = rows // slabs

        def slab_of(*ids, last=slabs - 1):
            step = sum(i * s for i, s in zip(ids, strides))
            return jnp.minimum(step, last)

        pass_in.append(pl.BlockSpec(
            (None, slab, cols),
            lambda *ids, f=slab_of, layer=p.layer: (layer, f(*ids), 0)))
        pass_out.append(pl.BlockSpec((slab, cols), lambda *ids, f=slab_of: (f(*ids), 0)))
        pass_shape.append(jax.ShapeDtypeStruct((rows, cols), BF16))

    def kern(*refs):
        ins = refs[:n_in]
        srcs = refs[n_in:n_in + n_pass]
        outs = refs[n_in + n_pass:n_in + n_pass + n_out]
        dsts = refs[n_in + n_pass + n_out:n_in + 2 * n_pass + n_out]
        scratch = refs[n_in + 2 * n_pass + n_out:]
        for s_ref, d_ref in zip(srcs, dsts):
            d_ref[...] = s_ref[...].astype(BF16)
        body(*ins, *outs, *scratch)

    res = pl.pallas_call(
        kern,
        grid=grid,
        in_specs=in_specs + pass_in,
        out_specs=out_specs + pass_out,
        out_shape=out_shape + pass_shape,
        scratch_shapes=list(scratch_shapes),
        compiler_params=pltpu.CompilerParams(
            dimension_semantics=("arbitrary",) * len(grid),
            vmem_limit_bytes=VMEM_LIMIT),
        name=name,
    )(*args, *[p.src for p in passengers])
    return res[:n_out], res[n_out:]


def _rms(x):
    return x * lax.rsqrt(jnp.mean(x * x, axis=-1, keepdims=True) + EPS)


def _prenorm_body(x_ref, w_ref, u_ref):
    u_ref[...] = (_rms(x_ref[...]) * w_ref[...]).astype(u_ref.dtype)


def _prenorm(x, w, rows=512):
    m, d = x.shape
    (u,), _ = _call(
        _prenorm_body,
        grid=(m // rows,),
        in_specs=[pl.BlockSpec((rows, d), lambda i: (i, 0)),
                  pl.BlockSpec((1, d), lambda i: (0, 0))],
        out_specs=[pl.BlockSpec((rows, d), lambda i: (i, 0))],
        out_shape=[jax.ShapeDtypeStruct((m, d), BF16)],
        args=[x, w.reshape(1, d)],
        name="prenorm")
    return u


def _post_body(h_ref, m_ref, wpost_ref, wpre_ref, hout_ref, u_ref):
    h = h_ref[...] + _rms(m_ref[...].astype(F32)) * wpost_ref[...]
    hout_ref[...] = h
    u_ref[...] = (_rms(h) * wpre_ref[...]).astype(u_ref.dtype)


def _post_last_body(h_ref, m_ref, wpost_ref, hout_ref):
    hout_ref[...] = h_ref[...] + _rms(m_ref[...].astype(F32)) * wpost_ref[...]


def _post(h, m_, wpost, wpre, rows=256):
    m, d = h.shape
    row_spec = pl.BlockSpec((rows, d), lambda i: (i, 0))
    w_spec = pl.BlockSpec((1, d), lambda i: (0, 0))
    if wpre is None:
        (h_new,), _ = _call(
            _post_last_body, grid=(m // rows,),
            in_specs=[row_spec, row_spec, w_spec], out_specs=[row_spec],
            out_shape=[jax.ShapeDtypeStruct((m, d), F32)],
            args=[h, m_, wpost.reshape(1, d)], name="post_last")
        return h_new, None
    (h_new, u), _ = _call(
        _post_body, grid=(m // rows,),
        in_specs=[row_spec, row_spec, w_spec, w_spec], out_specs=[row_spec, row_spec],
        out_shape=[jax.ShapeDtypeStruct((m, d), F32), jax.ShapeDtypeStruct((m, d), BF16)],
        args=[h, m_, wpost.reshape(1, d), wpre.reshape(1, d)], name="post")
    return h_new, u


_NT = (((1,), (1,)), ((), ()))


def _block_grid(m, n, bm, bn):
    assert m % bm == 0 and n % bn == 0, (m, n, bm, bn)
    return (m // bm, n // bn)


def _mm_body(a_ref, b_ref, o_ref):
    o_ref[...] = jnp.dot(a_ref[...], b_ref[...],
                         preferred_element_type=F32).astype(o_ref.dtype)


def _mm(a, b, out_dtype, bm, bn, name, passengers=()):
    m, k = a.shape
    n = b.shape[1]
    return _call(
        _mm_body,
        grid=_block_grid(m, n, bm, bn),
        in_specs=[pl.BlockSpec((bm, k), lambda i, j: (i, 0)),
                  pl.BlockSpec((k, bn), lambda i, j: (0, j))],
        out_specs=[pl.BlockSpec((bm, bn), lambda i, j: (i, j))],
        out_shape=[jax.ShapeDtypeStruct((m, n), out_dtype)],
        args=[a, b], name=name, passengers=passengers)


def _in_proj_body(a_ref, bt_ref, wfb_ref, o_ref, fb_ref):
    a = a_ref[...]
    o_ref[...] = lax.dot_general(a, bt_ref[...].astype(BF16), _NT,
                                 preferred_element_type=F32).astype(o_ref.dtype)

    @pl.when(pl.program_id(1) == 0)
    def _fb():
        wt = wfb_ref[...].astype(BF16)
        wt = jnp.concatenate(
            [wt, jnp.zeros((LANES - GATE_RANK, wt.shape[1]), BF16)], axis=0)
        fb_ref[...] = lax.dot_general(a, wt, _NT, preferred_element_type=F32)


def _in_proj(a, w_t, layer, bm, bn, passengers=()):
    m, k = a.shape
    return _call(
        _in_proj_body,
        grid=_block_grid(m, MAIN_COLS, bm, bn),
        in_specs=[pl.BlockSpec((bm, k), lambda i, j: (i, 0)),
                  pl.BlockSpec((None, bn, k), lambda i, j: (layer, j, 0)),
                  pl.BlockSpec((None, GATE_RANK, k),
                               lambda i, j: (layer, MAIN_COLS // GATE_RANK, 0))],
        out_specs=[pl.BlockSpec((bm, bn), lambda i, j: (i, j)),
                   pl.BlockSpec((bm, LANES), lambda i, j: (i, 0))],
        out_shape=[jax.ShapeDtypeStruct((m, MAIN_COLS), BF16),
                   jax.ShapeDtypeStruct((m, LANES), F32)],
        args=[a, w_t, w_t], name="in_proj", passengers=passengers)


def _mm2_body(a1_ref, a2_ref, b1_ref, b2_ref, o_ref):
    acc = jnp.dot(a1_ref[...], b1_ref[...], preferred_element_type=F32)
    acc += jnp.dot(a2_ref[...], b2_ref[...], preferred_element_type=F32)
    o_ref[...] = acc.astype(o_ref.dtype)


def _mm2(a1, a2, b, out_dtype, bm, bn, name):
    m, k1 = a1.shape
    _, k2 = a2.shape
    assert k1 == k2
    n = b.shape[1]
    (o,), _ = _call(
        _mm2_body,
        grid=_block_grid(m, n, bm, bn),
        in_specs=[pl.BlockSpec((bm, k1), lambda i, j: (i, 0)),
                  pl.BlockSpec((bm, k2), lambda i, j: (i, 0)),
                  pl.BlockSpec((k1, bn), lambda i, j: (0, j)),
                  pl.BlockSpec((k2, bn), lambda i, j: (1, j))],
        out_specs=[pl.BlockSpec((bm, bn), lambda i, j: (i, j))],
        out_shape=[jax.ShapeDtypeStruct((m, n), out_dtype)],
        args=[a1, a2, b, b], name=name)
    return o


def _gateup_body(a_ref, bg_ref, bu_ref, o_ref):
    bg = bg_ref[...].astype(BF16)
    bu = bu_ref[...].astype(BF16)
    rows_per = a_ref.shape[0] // GATEUP_ROW_SPLIT
    for r in range(GATEUP_ROW_SPLIT):
        rows = slice(r * rows_per, (r + 1) * rows_per)
        a = a_ref[rows, :]
        g = jnp.dot(a, bg, preferred_element_type=F32)
        u = jnp.dot(a, bu, preferred_element_type=F32)
        o_ref[rows, :] = (g * jax.nn.sigmoid(g) * u).astype(o_ref.dtype)


def _gateup(a, wg, wu, layer, bm, bn, passengers=()):
    m, k = a.shape
    n = wg.shape[2]
    w_spec = pl.BlockSpec((None, k, bn), lambda i, j: (layer, 0, j))
    return _call(
        _gateup_body,
        grid=_block_grid(m, n, bm, bn),
        in_specs=[pl.BlockSpec((bm, k), lambda i, j: (i, 0)), w_spec, w_spec],
        out_specs=[pl.BlockSpec((bm, bn), lambda i, j: (i, j))],
        out_shape=[jax.ShapeDtypeStruct((m, n), BF16)],
        args=[a, wg, wu], name="ffn_gateup", passengers=passengers)


def _attn_body(brow_ref, q_ref, k0_ref, k1_ref, k2_ref, v0_ref, v1_ref, v2_ref,
               o_ref, bias_ref):
    qi = pl.program_id(2)

    @pl.when(qi <= ATT_NKB - 1)
    def _build_bias():
        qc = lax.broadcasted_iota(jnp.int32, (ATT_TQ, ATT_TK), 0) // CHUNK
        col = lax.broadcasted_iota(jnp.int32, (ATT_TQ, ATT_TK), 1)
        kc = col // CHUNK
        first_valid = (ATT_NKB - 1 - qi) * ATT_TQ
        keep = (kc >= qc) & (kc <= qc + LEFT_CHUNKS) & (col >= first_valid)
        for h in range(ATT_HB):
            rows = jnp.broadcast_to(brow_ref[h] * LOG2E, (ATT_TQ, BIAS_W))
            rolled = pltpu.roll(rows, 0, 1, stride=1, stride_axis=0)
            bias_ref[h] = jnp.where(keep, rolled[:, :ATT_TK], NEG_INF)

    scale = A_HEAD_DIM ** -0.5 * LOG2E
    def scores(h):
        lanes = slice(h * A_HEAD_DIM, (h + 1) * A_HEAD_DIM)
        q = (q_ref[0, :, lanes].astype(F32) * scale).astype(BF16)
        k = jnp.concatenate([k0_ref[0, :, lanes], k1_ref[0, :, lanes],
                             k2_ref[0, :, lanes]], axis=0)
        return lax.dot_general(q, k, _NT, preferred_element_type=F32) + bias_ref[h]

    lane = lax.broadcasted_iota(jnp.int32, (ATT_TK, A_HEAD_DIM), 1)
    ones_col = jnp.where(lane == 0, 1.0, 0.0).astype(BF16)

    s_next = scores(0)
    for h in range(ATT_HB):
        lanes = slice(h * A_HEAD_DIM, (h + 1) * A_HEAD_DIM)
        s = s_next
        if h + 1 < ATT_HB:
            s_next = scores(h + 1)
        v = jnp.concatenate([v0_ref[0, :, lanes], v1_ref[0, :, lanes],
                             v2_ref[0, :, lanes]], axis=0)
        s_top = s[:ATT_TQ // 2, :ATT_TK - ATT_SKIP]
        s_bot = s[ATT_TQ // 2:, ATT_SKIP:]
        p_top = jnp.exp2(s_top - jnp.max(s_top, axis=-1, keepdims=True)).astype(BF16)
        p_bot = jnp.exp2(s_bot - jnp.max(s_bot, axis=-1, keepdims=True)).astype(BF16)
        zeros = jnp.zeros((ATT_TQ // 2, ATT_SKIP), BF16)
        p = jnp.concatenate([jnp.concatenate([p_top, zeros], axis=1),
                             jnp.concatenate([zeros, p_bot], axis=1)], axis=0)
        ol = jnp.dot(p, jnp.concatenate([v, ones_col], axis=1),
                     preferred_element_type=F32)
        o = ol[:, :A_HEAD_DIM]
        l = ol[:, A_HEAD_DIM:A_HEAD_DIM + 1]
        o_ref[0, :, lanes] = (o / l).astype(o_ref.dtype)


def _band_attention(proj3, brow, passengers=()):
    b, s, _ = proj3.shape
    nq = s // ATT_TQ
    ng = A_HEADS // ATT_HB
    gq = COL_QA // ATT_W
    gk = COL_KA // ATT_W
    gv = COL_VA // ATT_W

    def kv_spec(col0, back):
        return pl.BlockSpec(
            (1, ATT_TQ, ATT_W),
            lambda bi, g, qi: (bi, jnp.maximum(qi - back, 0), col0 + g))

    return _call(
        _attn_body,
        grid=(b, ng, nq),
        in_specs=[pl.BlockSpec((ATT_HB, 1, BIAS_W), lambda bi, g, qi: (g, 0, 0)),
                  pl.BlockSpec((1, ATT_TQ, ATT_W), lambda bi, g, qi: (bi, qi, gq + g)),
                  kv_spec(gk, 2), kv_spec(gk, 1), kv_spec(gk, 0),
                  kv_spec(gv, 2), kv_spec(gv, 1), kv_spec(gv, 0)],
        out_specs=[pl.BlockSpec((1, ATT_TQ, ATT_W), lambda bi, g, qi: (bi, qi, g))],
        out_shape=[jax.ShapeDtypeStruct((b, s, A_WIDTH), BF16)],
        scratch_shapes=[pltpu.VMEM((ATT_HB, ATT_TQ, ATT_TK), F32)],
        args=[brow] + [proj3] * 7, name="band_attention", passengers=passengers)


def _bias_rows(rel_bias):
    far = rel_bias[:, 2 * REL_CLIP:]
    near = rel_bias[:, :1]
    n_far = LEFT_CHUNKS * CHUNK - REL_CLIP
    n_near = ATT_TK - n_far - (2 * REL_CLIP + 1)
    row = jnp.concatenate([
        jnp.broadcast_to(far, (A_HEADS, n_far)),
        jnp.flip(rel_bias, axis=1),
        jnp.broadcast_to(near, (A_HEADS, n_near)),
        jnp.broadcast_to(far, (A_HEADS, BIAS_W - ATT_TK)),
    ], axis=1)
    return row.reshape(A_HEADS, 1, BIAS_W)


def _split2(x):
    hi = x.astype(BF16)
    lo = (x - hi.astype(F32)).astype(BF16)
    return hi, lo


def _gla_body(q_ref, k_ref, v_ref, g_ref, fb_ref, wfg_ref, bfg_ref, nw_ref,
              o_ref, st_ref):
    step = pl.program_id(1)

    @pl.when(step == 0)
    def _reset():
        st_ref[...] = jnp.zeros_like(st_ref)

    z = jnp.dot(fb_ref[0].astype(BF16), wfg_ref[...],
                preferred_element_type=F32) + bfg_ref[...]
    log_a = (jnp.minimum(z, 0.0) - jnp.log(1.0 + jnp.exp(-jnp.abs(z)))) / GATE_TEMP
    row = lax.broadcasted_iota(jnp.int32, (GLA_ROWS, GLA_ROWS), 0)
    colm = lax.broadcasted_iota(jnp.int32, (GLA_ROWS, GLA_ROWS), 1)
    causal = (colm <= row) & (colm // CHUNK == row // CHUNK)
    tri = jnp.where(causal, 1.0, 0.0).astype(BF16)
    hi, lo = _split2(log_a)
    cum = (jnp.dot(tri, hi, preferred_element_type=F32)
           + jnp.dot(tri, lo, preferred_element_type=F32))

    q_all = q_ref[0].astype(F32) * (B_HEAD_K ** -0.5)
    k_all = k_ref[0].astype(F32)
    q_dec_all = (q_all * jnp.exp(cum)).astype(BF16)
    k_inv_all = (k_all * jnp.exp(-cum)).astype(BF16)
    k_tail, e_last = [], []
    for c in range(GLA_ROWS // CHUNK):
        rows = slice(c * CHUNK, (c + 1) * CHUNK)
        last = cum[(c + 1) * CHUNK - 1:(c + 1) * CHUNK, :]
        k_tail.append((k_all[rows] * jnp.exp(last - cum[rows])).astype(BF16))
        e_last.append(jnp.exp(last))

    n_chunks = GLA_ROWS // CHUNK
    kls = [slice(h * B_HEAD_K, (h + 1) * B_HEAD_K) for h in range(B_HEADS)]
    vls = [slice(h * B_HEAD_V, (h + 1) * B_HEAD_V) for h in range(B_HEADS)]
    o_intra, upd = [], []
    for h in range(B_HEADS):
        v = v_ref[0, :, vls[h]]
        att = lax.dot_general(q_dec_all[:, kls[h]], k_inv_all[:, kls[h]], _NT,
                              preferred_element_type=F32)
        att = jnp.where(causal, att, 0.0).astype(BF16)
        o_intra.append(jnp.dot(att, v, preferred_element_type=F32))
        upd.append([lax.dot_general(v[c * CHUNK:(c + 1) * CHUNK],
                                    k_tail[c][:, kls[h]], (((0,), (0,)), ((), ())),
                                    preferred_element_type=F32)
                    for c in range(n_chunks)])

    st = [st_ref[h] for h in range(B_HEADS)]
    o_parts = [[] for _ in range(B_HEADS)]
    for c in range(n_chunks):
        rows = slice(c * CHUNK, (c + 1) * CHUNK)
        for h in range(B_HEADS):
            o_parts[h].append(o_intra[h][rows] + lax.dot_general(
                q_dec_all[rows, kls[h]], st[h].astype(BF16), _NT,
                preferred_element_type=F32))
            st[h] = st[h] * e_last[c][:, kls[h]] + upd[h][c]

    for h in range(B_HEADS):
        st_ref[h] = st[h]
        o = _rms(jnp.concatenate(o_parts[h], axis=0)) * nw_ref[...]
        g = g_ref[0, :, vls[h]].astype(F32)
        o_ref[0, :, vls[h]] = (o * (g * jax.nn.sigmoid(g))).astype(o_ref.dtype)


def _gla(proj3, fb3, wfg, bfg, nw, passengers=()):
    b, s, _ = proj3.shape
    cq = COL_QB // B_KEY_WIDTH
    ck = COL_KB // B_KEY_WIDTH
    cv = COL_VB // B_WIDTH
    cg = COL_GB // B_WIDTH
    return _call(
        _gla_body,
        grid=(b, s // GLA_ROWS),
        in_specs=[
            pl.BlockSpec((1, GLA_ROWS, B_KEY_WIDTH), lambda bi, c: (bi, c, cq)),
            pl.BlockSpec((1, GLA_ROWS, B_KEY_WIDTH), lambda bi, c: (bi, c, ck)),
            pl.BlockSpec((1, GLA_ROWS, B_WIDTH), lambda bi, c: (bi, c, cv)),
            pl.BlockSpec((1, GLA_ROWS, B_WIDTH), lambda bi, c: (bi, c, cg)),
            pl.BlockSpec((1, GLA_ROWS, LANES), lambda bi, c: (bi, c, 0)),
            pl.BlockSpec((LANES, B_KEY_WIDTH), lambda bi, c: (0, 0)),
            pl.BlockSpec((1, B_KEY_WIDTH), lambda bi, c: (0, 0)),
            pl.BlockSpec((1, B_HEAD_V), lambda bi, c: (0, 0)),
        ],
        out_specs=[pl.BlockSpec((1, GLA_ROWS, B_WIDTH), lambda bi, c: (bi, c, 0))],
        out_shape=[jax.ShapeDtypeStruct((b, s, B_WIDTH), BF16)],
        scratch_shapes=[pltpu.VMEM((B_HEADS, B_HEAD_V, B_HEAD_K), F32)],
        args=[proj3, proj3, proj3, proj3, fb3, wfg, bfg, nw], name="gla",
        passengers=passengers)


def kernel(x, norm_mix_pre, norm_mix_post, norm_ffn_pre, norm_ffn_post, w_in, rel_bias,
           w_fgate_up, b_fgate, gla_norm, w_out, w_ffn_gate, w_ffn_up, w_ffn_down):
    b, s, d = x.shape
    depth = w_in.shape[0]
    m = b * s
    w_fg = jnp.pad(w_fgate_up, ((0, 0), (0, LANES - GATE_RANK), (0, 0))).astype(BF16)
    w_in_t = jnp.swapaxes(w_in, 1, 2)
    h = x.reshape(m, d)
    u = _prenorm(h, norm_mix_pre[0])
    for i in range(depth):
        (proj, fb), _ = _in_proj(u, w_in_t, i, *IN_PROJ_BLOCK)
        proj3 = proj.reshape(b, s, MAIN_COLS)
        (ya,), _ = _band_attention(proj3, _bias_rows(rel_bias[i]))
        (yb,), (wout_b,) = _gla(proj3, fb.reshape(b, s, LANES), w_fg[i],
                                b_fgate[i].reshape(1, B_KEY_WIDTH),
                                gla_norm[i].reshape(1, B_HEAD_V),
                                passengers=[Passenger(w_out, i)])
        mix = _mm2(ya.reshape(m, A_WIDTH), yb.reshape(m, B_WIDTH), wout_b,
                   BF16, *OUT_PROJ_BLOCK, "out_proj")
        h, u = _post(h, mix, norm_mix_post[i], norm_ffn_pre[i])
        (hid,), (wdown_b,) = _gateup(u, w_ffn_gate, w_ffn_up, i, *GATEUP_BLOCK,
                                     passengers=[Passenger(w_ffn_down, i)])
        (f,), _ = _mm(hid, wdown_b, BF16, *DOWN_BLOCK, "ffn_down")
        h, u = _post(h, f, norm_ffn_post[i], norm_mix_pre[i + 1] if i + 1 < depth else None)
    return h.reshape(b, s, d)
```

```python
from typing import NamedTuple

import jax
import jax.numpy as jnp
from jax import lax
from jax.experimental import pallas as pl
from jax.experimental.pallas import tpu as pltpu

CHUNK = 64
LEFT_CHUNKS = 8
A_WIDTH = 2048
A_HEAD_DIM = 128
A_HEADS = A_WIDTH // A_HEAD_DIM
REL_CLIP = 128
B_WIDTH = 2048
B_HEADS = 4
B_HEAD_V = B_WIDTH // B_HEADS
B_KEY_WIDTH = B_WIDTH // 2
B_HEAD_K = B_KEY_WIDTH // B_HEADS
GATE_RANK = 16
GATE_TEMP = 16.0
EPS = 1e-6
NEG_INF = -1e30
LOG2E = 1.4426950408889634

LANES = 128
BF16_SUBLANES = 16
VMEM_LIMIT = 60 * 1024 * 1024

IN_PROJ_BLOCK = (1024, 768)
OUT_PROJ_BLOCK = (1024, 1024)
GATEUP_BLOCK = (2048, 256)
DOWN_BLOCK = (512, 512)
GATEUP_ROW_SPLIT = 2

COL_QA = 0
COL_KA = A_WIDTH
COL_VA = 2 * A_WIDTH
COL_QB = 3 * A_WIDTH
COL_KB = COL_QB + B_KEY_WIDTH
COL_VB = COL_KB + B_KEY_WIDTH
COL_GB = COL_VB + B_WIDTH
MAIN_COLS = COL_GB + B_WIDTH

ATT_TQ = 4 * CHUNK
ATT_NKB = LEFT_CHUNKS * CHUNK // ATT_TQ + 1
ATT_TK = ATT_NKB * ATT_TQ
ATT_HB = A_HEADS
ATT_W = ATT_HB * A_HEAD_DIM
BIAS_W = ATT_TK + ATT_TQ
ATT_SKIP = ATT_TQ // 2
GLA_ROWS = 4 * CHUNK

BF16 = jnp.bfloat16
F32 = jnp.float32


class Passenger(NamedTuple):
    src: jax.Array
    layer: int


def _slab_count(rows, n_steps):
    tiles = rows // BF16_SUBLANES
    assert tiles * BF16_SUBLANES == rows
    return max(s for s in range(1, min(tiles, n_steps) + 1) if tiles % s == 0)


def _call(body, *, grid, in_specs, out_specs, out_shape, args, name,
          scratch_shapes=(), passengers=()):
    n_in, n_out, n_pass = len(in_specs), len(out_specs), len(passengers)
    strides = [1] * len(grid)
    for ax in range(len(grid) - 2, -1, -1):
        strides[ax] = strides[ax + 1] * grid[ax + 1]
    n_steps = strides[0] * grid[0]

    in_specs, out_specs, out_shape, args = (list(in_specs), list(out_specs),
                                            list(out_shape), list(args))
    pass_in, pass_out, pass_shape = [], [], []
    for p in passengers:
        _, rows, cols = p.src.shape
        slabs = _slab_count(rows, n_steps)
        slab = rows // slabs

        def slab_of(*ids, last=slabs - 1):
            step = sum(i * s for i, s in zip(ids, strides))
            return jnp.minimum(step, last)

        pass_in.append(pl.BlockSpec(
            (None, slab, cols),
            lambda *ids, f=slab_of, layer=p.layer: (layer, f(*ids), 0)))
        pass_out.append(pl.BlockSpec((slab, cols), lambda *ids, f=slab_of: (f(*ids), 0)))
        pass_shape.append(jax.ShapeDtypeStruct((rows, cols), BF16))

    def kern(*refs):
        ins = refs[:n_in]
        srcs = refs[n_in:n_in + n_pass]
        outs = refs[n_in + n_pass:n_in + n_pass + n_out]
        dsts = refs[n_in + n_pass + n_out:n_in + 2 * n_pass + n_out]
        scratch = refs[n_in + 2 * n_pass + n_out:]
        for s_ref, d_ref in zip(srcs, dsts):
            d_ref[...] = s_ref[...].astype(BF16)
        body(*ins, *outs, *scratch)

    res = pl.pallas_call(
        kern,
        grid=grid,
        in_specs=in_specs + pass_in,
        out_specs=out_specs + pass_out,
        out_shape=out_shape + pass_shape,
        scratch_shapes=list(scratch_shapes),
        compiler_params=pltpu.CompilerParams(
            dimension_semantics=("arbitrary",) * len(grid),
            vmem_limit_bytes=VMEM_LIMIT),
        name=name,
    )(*args, *[p.src for p in passengers])
    return res[:n_out], res[n_out:]


def _rms(x):
    return x * lax.rsqrt(jnp.mean(x * x, axis=-1, keepdims=True) + EPS)


def _prenorm_body(x_ref, w_ref, u_ref):
    u_ref[...] = (_rms(x_ref[...]) * w_ref[...]).astype(u_ref.dtype)


def _prenorm(x, w, rows=512):
    m, d = x.shape
    (u,), _ = _call(
        _prenorm_body,
        grid=(m // rows,),
        in_specs=[pl.BlockSpec((rows, d), lambda i: (i, 0)),
                  pl.BlockSpec((1, d), lambda i: (0, 0))],
        out_specs=[pl.BlockSpec((rows, d), lambda i: (i, 0))],
        out_shape=[jax.ShapeDtypeStruct((m, d), BF16)],
        args=[x, w.reshape(1, d)],
        name="prenorm")
    return u


def _post_body(h_ref, m_ref, wpost_ref, wpre_ref, hout_ref, u_ref):
    h = h_ref[...] + _rms(m_ref[...].astype(F32)) * wpost_ref[...]
    hout_ref[...] = h
    u_ref[...] = (_rms(h) * wpre_ref[...]).astype(u_ref.dtype)


def _post_last_body(h_ref, m_ref, wpost_ref, hout_ref):
    hout_ref[...] = h_ref[...] + _rms(m_ref[...].astype(F32)) * wpost_ref[...]


def _post(h, m_, wpost, wpre, rows=256):
    m, d = h.shape
    row_spec = pl.BlockSpec((rows, d), lambda i: (i, 0))
    w_spec = pl.BlockSpec((1, d), lambda i: (0, 0))
    if wpre is None:
        (h_new,), _ = _call(
            _post_last_body, grid=(m // rows,),
            in_specs=[row_spec, row_spec, w_spec], out_specs=[row_spec],
            out_shape=[jax.ShapeDtypeStruct((m, d), F32)],
            args=[h, m_, wpost.reshape(1, d)], name="post_last")
        return h_new, None
    (h_new, u), _ = _call(
        _post_body, grid=(m // rows,),
        in_specs=[row_spec, row_spec, w_spec, w_spec], out_specs=[row_spec, row_spec],
        out_shape=[jax.ShapeDtypeStruct((m, d), F32), jax.ShapeDtypeStruct((m, d), BF16)],
        args=[h, m_, wpost.reshape(1, d), wpre.reshape(1, d)], name="post")
    return h_new, u


_NT = (((1,), (1,)), ((), ()))


def _block_grid(m, n, bm, bn):
    assert m % bm == 0 and n % bn == 0, (m, n, bm, bn)
    return (m // bm, n // bn)


def _mm_body(a_ref, b_ref, o_ref):
    o_ref[...] = jnp.dot(a_ref[...], b_ref[...],
                         preferred_element_type=F32).astype(o_ref.dtype)


def _mm(a, b, out_dtype, bm, bn, name):
    m, k = a.shape
    n = b.shape[1]
    (o,), _ = _call(
        _mm_body,
        grid=_block_grid(m, n, bm, bn),
        in_specs=[pl.BlockSpec((bm, k), lambda i, j: (i, 0)),
                  pl.BlockSpec((k, bn), lambda i, j: (0, j))],
        out_specs=[pl.BlockSpec((bm, bn), lambda i, j: (i, j))],
        out_shape=[jax.ShapeDtypeStruct((m, n), out_dtype)],
        args=[a, b], name=name)
    return o


def _in_proj_body(a_ref, bt_ref, wfb_ref, o_ref, fb_ref):
    a = a_ref[...]
    o_ref[...] = lax.dot_general(a, bt_ref[...].astype(BF16), _NT,
                                 preferred_element_type=F32).astype(o_ref.dtype)

    @pl.when(pl.program_id(1) == 0)
    def _fb():
        wt = wfb_ref[...].astype(BF16)
        wt = jnp.concatenate(
            [wt, jnp.zeros((LANES - GATE_RANK, wt.shape[1]), BF16)], axis=0)
        fb_ref[...] = lax.dot_general(a, wt, _NT, preferred_element_type=F32)


def _in_proj(a, w_t, layer, bm, bn):
    m, k = a.shape
    (proj, fb), _ = _call(
        _in_proj_body,
        grid=_block_grid(m, MAIN_COLS, bm, bn),
        in_specs=[pl.BlockSpec((bm, k), lambda i, j: (i, 0)),
                  pl.BlockSpec((None, bn, k), lambda i, j: (layer, j, 0)),
                  pl.BlockSpec((None, GATE_RANK, k),
                               lambda i, j: (layer, MAIN_COLS // GATE_RANK, 0))],
        out_specs=[pl.BlockSpec((bm, bn), lambda i, j: (i, j)),
                   pl.BlockSpec((bm, LANES), lambda i, j: (i, 0))],
        out_shape=[jax.ShapeDtypeStruct((m, MAIN_COLS), BF16),
                   jax.ShapeDtypeStruct((m, LANES), F32)],
        args=[a, w_t, w_t], name="in_proj")
    return proj, fb


def _mm2_body(a1_ref, a2_ref, b1_ref, b2_ref, o_ref):
    acc = jnp.dot(a1_ref[...], b1_ref[...], preferred_element_type=F32)
    acc += jnp.dot(a2_ref[...], b2_ref[...], preferred_element_type=F32)
    o_ref[...] = acc.astype(o_ref.dtype)


def _mm2(a1, a2, b, out_dtype, bm, bn, name):
    m, k1 = a1.shape
    _, k2 = a2.shape
    assert k1 == k2
    n = b.shape[1]
    (o,), _ = _call(
        _mm2_body,
        grid=_block_grid(m, n, bm, bn),
        in_specs=[pl.BlockSpec((bm, k1), lambda i, j: (i, 0)),
                  pl.BlockSpec((bm, k2), lambda i, j: (i, 0)),
                  pl.BlockSpec((k1, bn), lambda i, j: (0, j)),
                  pl.BlockSpec((k2, bn), lambda i, j: (1, j))],
        out_specs=[pl.BlockSpec((bm, bn), lambda i, j: (i, j))],
        out_shape=[jax.ShapeDtypeStruct((m, n), out_dtype)],
        args=[a1, a2, b, b], name=name)
    return o


def _gateup_body(a_ref, bg_ref, bu_ref, o_ref):
    bg = bg_ref[...].astype(BF16)
    bu = bu_ref[...].astype(BF16)
    rows_per = a_ref.shape[0] // GATEUP_ROW_SPLIT
    for r in range(GATEUP_ROW_SPLIT):
        rows = slice(r * rows_per, (r + 1) * rows_per)
        a = a_ref[rows, :]
        g = jnp.dot(a, bg, preferred_element_type=F32)
        u = jnp.dot(a, bu, preferred_element_type=F32)
        o_ref[rows, :] = (g * jax.nn.sigmoid(g) * u).astype(o_ref.dtype)


def _gateup(a, wg, wu, layer, bm, bn, passengers=()):
    m, k = a.shape
    n = wg.shape[2]
    w_spec = pl.BlockSpec((None, k, bn), lambda i, j: (layer, 0, j))
    return _call(
        _gateup_body,
        grid=_block_grid(m, n, bm, bn),
        in_specs=[pl.BlockSpec((bm, k), lambda i, j: (i, 0)), w_spec, w_spec],
        out_specs=[pl.BlockSpec((bm, bn), lambda i, j: (i, j))],
        out_shape=[jax.ShapeDtypeStruct((m, n), BF16)],
        args=[a, wg, wu], name="ffn_gateup", passengers=passengers)


def _attn_body(brow_ref, q_ref, k0_ref, k1_ref, k2_ref, v0_ref, v1_ref, v2_ref,
               o_ref, bias_ref):
    qi = pl.program_id(2)

    @pl.when(qi <= ATT_NKB - 1)
    def _build_bias():
        qc = lax.broadcasted_iota(jnp.int32, (ATT_TQ, ATT_TK), 0) // CHUNK
        col = lax.broadcasted_iota(jnp.int32, (ATT_TQ, ATT_TK), 1)
        kc = col // CHUNK
        first_valid = (ATT_NKB - 1 - qi) * ATT_TQ
        keep = (kc >= qc) & (kc <= qc + LEFT_CHUNKS) & (col >= first_valid)
        for h in range(ATT_HB):
            rows = jnp.broadcast_to(brow_ref[h] * LOG2E, (ATT_TQ, BIAS_W))
            rolled = pltpu.roll(rows, 0, 1, stride=1, stride_axis=0)
            bias_ref[h] = jnp.where(keep, rolled[:, :ATT_TK], NEG_INF)

    scale = A_HEAD_DIM ** -0.5 * LOG2E
    def scores(h):
        lanes = slice(h * A_HEAD_DIM, (h + 1) * A_HEAD_DIM)
        q = (q_ref[0, :, lanes].astype(F32) * scale).astype(BF16)
        k = jnp.concatenate([k0_ref[0, :, lanes], k1_ref[0, :, lanes],
                             k2_ref[0, :, lanes]], axis=0)
        return lax.dot_general(q, k, _NT, preferred_element_type=F32) + bias_ref[h]

    lane = lax.broadcasted_iota(jnp.int32, (ATT_TK, A_HEAD_DIM), 1)
    ones_col = jnp.where(lane == 0, 1.0, 0.0).astype(BF16)

    s_next = scores(0)
    for h in range(ATT_HB):
        lanes = slice(h * A_HEAD_DIM, (h + 1) * A_HEAD_DIM)
        s = s_next
        if h + 1 < ATT_HB:
            s_next = scores(h + 1)
        v = jnp.concatenate([v0_ref[0, :, lanes], v1_ref[0, :, lanes],
                             v2_ref[0, :, lanes]], axis=0)
        s_top = s[:ATT_TQ // 2, :ATT_TK - ATT_SKIP]
        s_bot = s[ATT_TQ // 2:, ATT_SKIP:]
        p_top = jnp.exp2(s_top - jnp.max(s_top, axis=-1, keepdims=True)).astype(BF16)
        p_bot = jnp.exp2(s_bot - jnp.max(s_bot, axis=-1, keepdims=True)).astype(BF16)
        zeros = jnp.zeros((ATT_TQ // 2, ATT_SKIP), BF16)
        p = jnp.concatenate([jnp.concatenate([p_top, zeros], axis=1),
                             jnp.concatenate([zeros, p_bot], axis=1)], axis=0)
        ol = jnp.dot(p, jnp.concatenate([v, ones_col], axis=1),
                     preferred_element_type=F32)
        o = ol[:, :A_HEAD_DIM]
        l = ol[:, A_HEAD_DIM:A_HEAD_DIM + 1]
        o_ref[0, :, lanes] = (o / l).astype(o_ref.dtype)


def _band_attention(proj3, brow):
    b, s, _ = proj3.shape
    nq = s // ATT_TQ
    ng = A_HEADS // ATT_HB
    gq = COL_QA // ATT_W
    gk = COL_KA // ATT_W
    gv = COL_VA // ATT_W

    def kv_spec(col0, back):
        return pl.BlockSpec(
            (1, ATT_TQ, ATT_W),
            lambda bi, g, qi: (bi, jnp.maximum(qi - back, 0), col0 + g))

    (ya,), _ = _call(
        _attn_body,
        grid=(b, ng, nq),
        in_specs=[pl.BlockSpec((ATT_HB, 1, BIAS_W), lambda bi, g, qi: (g, 0, 0)),
                  pl.BlockSpec((1, ATT_TQ, ATT_W), lambda bi, g, qi: (bi, qi, gq + g)),
                  kv_spec(gk, 2), kv_spec(gk, 1), kv_spec(gk, 0),
                  kv_spec(gv, 2), kv_spec(gv, 1), kv_spec(gv, 0)],
        out_specs=[pl.BlockSpec((1, ATT_TQ, ATT_W), lambda bi, g, qi: (bi, qi, g))],
        out_shape=[jax.ShapeDtypeStruct((b, s, A_WIDTH), BF16)],
        scratch_shapes=[pltpu.VMEM((ATT_HB, ATT_TQ, ATT_TK), F32)],
        args=[brow] + [proj3] * 7, name="band_attention")
    return ya


def _bias_rows(rel_bias):
    far = rel_bias[:, 2 * REL_CLIP:]
    near = rel_bias[:, :1]
    n_far = LEFT_CHUNKS * CHUNK - REL_CLIP
    n_near = ATT_TK - n_far - (2 * REL_CLIP + 1)
    row = jnp.concatenate([
        jnp.broadcast_to(far, (A_HEADS, n_far)),
        jnp.flip(rel_bias, axis=1),
        jnp.broadcast_to(near, (A_HEADS, n_near)),
        jnp.broadcast_to(far, (A_HEADS, BIAS_W - ATT_TK)),
    ], axis=1)
    return row.reshape(A_HEADS, 1, BIAS_W)


def _split2(x):
    hi = x.astype(BF16)
    lo = (x - hi.astype(F32)).astype(BF16)
    return hi, lo


def _gla_body(q_ref, k_ref, v_ref, g_ref, fb_ref, wfg_ref, bfg_ref, nw_ref,
              o_ref, st_ref):
    step = pl.program_id(1)

    @pl.when(step == 0)
    def _reset():
        st_ref[...] = jnp.zeros_like(st_ref)

    z = jnp.dot(fb_ref[0].astype(BF16), wfg_ref[...],
                preferred_element_type=F32) + bfg_ref[...]
    log_a = (jnp.minimum(z, 0.0) - jnp.log(1.0 + jnp.exp(-jnp.abs(z)))) / GATE_TEMP
    row = lax.broadcasted_iota(jnp.int32, (GLA_ROWS, GLA_ROWS), 0)
    colm = lax.broadcasted_iota(jnp.int32, (GLA_ROWS, GLA_ROWS), 1)
    causal = (colm <= row) & (colm // CHUNK == row // CHUNK)
    tri = jnp.where(causal, 1.0, 0.0).astype(BF16)
    hi, lo = _split2(log_a)
    cum = (jnp.dot(tri, hi, preferred_element_type=F32)
           + jnp.dot(tri, lo, preferred_element_type=F32))

    q_all = q_ref[0].astype(F32) * (B_HEAD_K ** -0.5)
    k_all = k_ref[0].astype(F32)
    q_dec_all = (q_all * jnp.exp(cum)).astype(BF16)
    k_inv_all = (k_all * jnp.exp(-cum)).astype(BF16)
    k_tail, e_last = [], []
    for c in range(GLA_ROWS // CHUNK):
        rows = slice(c * CHUNK, (c + 1) * CHUNK)
        last = cum[(c + 1) * CHUNK - 1:(c + 1) * CHUNK, :]
        k_tail.append((k_all[rows] * jnp.exp(last - cum[rows])).astype(BF16))
        e_last.append(jnp.exp(last))

    n_chunks = GLA_ROWS // CHUNK
    kls = [slice(h * B_HEAD_K, (h + 1) * B_HEAD_K) for h in range(B_HEADS)]
    vls = [slice(h * B_HEAD_V, (h + 1) * B_HEAD_V) for h in range(B_HEADS)]
    o_intra, upd = [], []
    for h in range(B_HEADS):
        v = v_ref[0, :, vls[h]]
        att = lax.dot_general(q_dec_all[:, kls[h]], k_inv_all[:, kls[h]], _NT,
                              preferred_element_type=F32)
        att = jnp.where(causal, att, 0.0).astype(BF16)
        o_intra.append(jnp.dot(att, v, preferred_element_type=F32))
        upd.append([lax.dot_general(v[c * CHUNK:(c + 1) * CHUNK],
                                    k_tail[c][:, kls[h]], (((0,), (0,)), ((), ())),
                                    preferred_element_type=F32)
                    for c in range(n_chunks)])

    st = [st_ref[h] for h in range(B_HEADS)]
    o_parts = [[] for _ in range(B_HEADS)]
    for c in range(n_chunks):
        rows = slice(c * CHUNK, (c + 1) * CHUNK)
        for h in range(B_HEADS):
            o_parts[h].append(o_intra[h][rows] + lax.dot_general(
                q_dec_all[rows, kls[h]], st[h].astype(BF16), _NT,
                preferred_element_type=F32))
            st[h] = st[h] * e_last[c][:, kls[h]] + upd[h][c]

    for h in range(B_HEADS):
        st_ref[h] = st[h]
        o = _rms(jnp.concatenate(o_parts[h], axis=0)) * nw_ref[...]
        g = g_ref[0, :, vls[h]].astype(F32)
        o_ref[0, :, vls[h]] = (o * (g * jax.nn.sigmoid(g))).astype(o_ref.dtype)


def _gla(proj3, fb3, wfg, bfg, nw, passengers=()):
    b, s, _ = proj3.shape
    cq = COL_QB // B_KEY_WIDTH
    ck = COL_KB // B_KEY_WIDTH
    cv = COL_VB // B_WIDTH
    cg = COL_GB // B_WIDTH
    return _call(
        _gla_body,
        grid=(b, s // GLA_ROWS),
        in_specs=[
            pl.BlockSpec((1, GLA_ROWS, B_KEY_WIDTH), lambda bi, c: (bi, c, cq)),
            pl.BlockSpec((1, GLA_ROWS, B_KEY_WIDTH), lambda bi, c: (bi, c, ck)),
            pl.BlockSpec((1, GLA_ROWS, B_WIDTH), lambda bi, c: (bi, c, cv)),
            pl.BlockSpec((1, GLA_ROWS, B_WIDTH), lambda bi, c: (bi, c, cg)),
            pl.BlockSpec((1, GLA_ROWS, LANES), lambda bi, c: (bi, c, 0)),
            pl.BlockSpec((LANES, B_KEY_WIDTH), lambda bi, c: (0, 0)),
            pl.BlockSpec((1, B_KEY_WIDTH), lambda bi, c: (0, 0)),
            pl.BlockSpec((1, B_HEAD_V), lambda bi, c: (0, 0)),
        ],
        out_specs=[pl.BlockSpec((1, GLA_ROWS, B_WIDTH), lambda bi, c: (bi, c, 0))],
        out_shape=[jax.ShapeDtypeStruct((b, s, B_WIDTH), BF16)],
        scratch_shapes=[pltpu.VMEM((B_HEADS, B_HEAD_V, B_HEAD_K), F32)],
        args=[proj3, proj3, proj3, proj3, fb3, wfg, bfg, nw], name="gla",
        passengers=passengers)


def kernel(x, norm_mix_pre, norm_mix_post, norm_ffn_pre, norm_ffn_post, w_in, rel_bias,
           w_fgate_up, b_fgate, gla_norm, w_out, w_ffn_gate, w_ffn_up, w_ffn_down):
    b, s, d = x.shape
    depth = w_in.shape[0]
    m = b * s
    w_fg = jnp.pad(w_fgate_up, ((0, 0), (0, LANES - GATE_RANK), (0, 0))).astype(BF16)
    w_in_t = jnp.swapaxes(w_in, 1, 2)
    h = x.reshape(m, d)
    u = _prenorm(h, norm_mix_pre[0])
    for i in range(depth):
        proj, fb = _in_proj(u, w_in_t, i, *IN_PROJ_BLOCK)
        proj3 = proj.reshape(b, s, MAIN_COLS)
        ya = _band_attention(proj3, _bias_rows(rel_bias[i]))
        (yb,), (wout_b,) = _gla(proj3, fb.reshape(b, s, LANES), w_fg[i],
                                b_fgate[i].reshape(1, B_KEY_WIDTH),
                                gla_norm[i].reshape(1, B_HEAD_V),
                                passengers=[Passenger(w_out, i)])
        mix = _mm2(ya.reshape(m, A_WIDTH), yb.reshape(m, B_WIDTH), wout_b,
                   BF16, *OUT_PROJ_BLOCK, "out_proj")
        h, u = _post(h, mix, norm_mix_post[i], norm_ffn_pre[i])
        (hid,), (wdown_b,) = _gateup(u, w_ffn_gate, w_ffn_up, i, *GATEUP_BLOCK,
                                     passengers=[Passenger(w_ffn_down, i)])
        f = _mm(hid, wdown_b, BF16, *DOWN_BLOCK, "ffn_down")
        h, u = _post(h, f, norm_ffn_post[i], norm_mix_pre[i + 1] if i + 1 < depth else None)
    return h.reshape(b, s, d)
```

```python
from typing import NamedTuple

import jax
import jax.numpy as jnp
from jax import lax
from jax.experimental import pallas as pl
from jax.experimental.pallas import tpu as pltpu

CHUNK = 64
LEFT_CHUNKS = 8
A_WIDTH = 2048
A_HEAD_DIM = 128
A_HEADS = A_WIDTH // A_HEAD_DIM
REL_CLIP = 128
B_WIDTH = 2048
B_HEADS = 4
B_HEAD_V = B_WIDTH // B_HEADS
B_KEY_WIDTH = B_WIDTH // 2
B_HEAD_K = B_KEY_WIDTH // B_HEADS
GATE_RANK = 16
GATE_TEMP = 16.0
EPS = 1e-6
NEG_INF = -1e30
LOG2E = 1.4426950408889634

LANES = 128
BF16_SUBLANES = 16
VMEM_LIMIT = 60 * 1024 * 1024

IN_PROJ_BLOCK = (1024, 768)
OUT_PROJ_BLOCK = (1024, 1024)
GATEUP_BLOCK = (2048, 256)
DOWN_BLOCK = (1024, 256)
GATEUP_ROW_SPLIT = 2

COL_QA = 0
COL_KA = A_WIDTH
COL_VA = 2 * A_WIDTH
COL_QB = 3 * A_WIDTH
COL_KB = COL_QB + B_KEY_WIDTH
COL_VB = COL_KB + B_KEY_WIDTH
COL_GB = COL_VB + B_WIDTH
MAIN_COLS = COL_GB + B_WIDTH

ATT_TQ = 4 * CHUNK
ATT_NKB = LEFT_CHUNKS * CHUNK // ATT_TQ + 1
ATT_TK = ATT_NKB * ATT_TQ
ATT_HB = A_HEADS
ATT_W = ATT_HB * A_HEAD_DIM
BIAS_W = ATT_TK + ATT_TQ
ATT_SKIP = ATT_TQ // 2
GLA_ROWS = 4 * CHUNK

BF16 = jnp.bfloat16
F32 = jnp.float32


class Passenger(NamedTuple):
    src: jax.Array
    layer: int


def _slab_count(rows, n_steps):
    tiles = rows // BF16_SUBLANES
    assert tiles * BF16_SUBLANES == rows
    return max(s for s in range(1, min(tiles, n_steps) + 1) if tiles % s == 0)


def _call(body, *, grid, in_specs, out_specs, out_shape, args, name,
          scratch_shapes=(), passengers=()):
    n_in, n_out, n_pass = len(in_specs), len(out_specs), len(passengers)
    strides = [1] * len(grid)
    for ax in range(len(grid) - 2, -1, -1):
        strides[ax] = strides[ax + 1] * grid[ax + 1]
    n_steps = strides[0] * grid[0]

    in_specs, out_specs, out_shape, args = (list(in_specs), list(out_specs),
                                            list(out_shape), list(args))
    pass_in, pass_out, pass_shape = [], [], []
    for p in passengers:
        _, rows, cols = p.src.shape
        slabs = _slab_count(rows, n_steps)
        slab = rows // slabs

        def slab_of(*ids, last=slabs - 1):
            step = sum(i * s for i, s in zip(ids, strides))
            return jnp.minimum(step, last)

        pass_in.append(pl.BlockSpec(
            (None, slab, cols),
            lambda *ids, f=slab_of, layer=p.layer: (layer, f(*ids), 0)))
        pass_out.append(pl.BlockSpec((slab, cols), lambda *ids, f=slab_of: (f(*ids), 0)))
        pass_shape.append(jax.ShapeDtypeStruct((rows, cols), BF16))

    def kern(*refs):
        ins = refs[:n_in]
        srcs = refs[n_in:n_in + n_pass]
        outs = refs[n_in + n_pass:n_in + n_pass + n_out]
        dsts = refs[n_in + n_pass + n_out:n_in + 2 * n_pass + n_out]
        scratch = refs[n_in + 2 * n_pass + n_out:]
        for s_ref, d_ref in zip(srcs, dsts):
            d_ref[...] = s_ref[...].astype(BF16)
        body(*ins, *outs, *scratch)

    res = pl.pallas_call(
        kern,
        grid=grid,
        in_specs=in_specs + pass_in,
        out_specs=out_specs + pass_out,
        out_shape=out_shape + pass_shape,
        scratch_shapes=list(scratch_shapes),
        compiler_params=pltpu.CompilerParams(
            dimension_semantics=("arbitrary",) * len(grid),
            vmem_limit_bytes=VMEM_LIMIT),
        name=name,
    )(*args, *[p.src for p in passengers])
    return res[:n_out], res[n_out:]


def _rms(x):
    return x * lax.rsqrt(jnp.mean(x * x, axis=-1, keepdims=True) + EPS)


def _prenorm_body(x_ref, w_ref, u_ref):
    u_ref[...] = (_rms(x_ref[...]) * w_ref[...]).astype(u_ref.dtype)


def _prenorm(x, w, rows=512):
    m, d = x.shape
    (u,), _ = _call(
        _prenorm_body,
        grid=(m // rows,),
        in_specs=[pl.BlockSpec((rows, d), lambda i: (i, 0)),
                  pl.BlockSpec((1, d), lambda i: (0, 0))],
        out_specs=[pl.BlockSpec((rows, d), lambda i: (i, 0))],
        out_shape=[jax.ShapeDtypeStruct((m, d), BF16)],
        args=[x, w.reshape(1, d)],
        name="prenorm")
    return u


def _post_body(h_ref, m_ref, wpost_ref, wpre_ref, hout_ref, u_ref):
    h = h_ref[...] + _rms(m_ref[...].astype(F32)) * wpost_ref[...]
    hout_ref[...] = h
    u_ref[...] = (_rms(h) * wpre_ref[...]).astype(u_ref.dtype)


def _post_last_body(h_ref, m_ref, wpost_ref, hout_ref):
    hout_ref[...] = h_ref[...] + _rms(m_ref[...].astype(F32)) * wpost_ref[...]


def _post(h, m_, wpost, wpre, rows=256):
    m, d = h.shape
    row_spec = pl.BlockSpec((rows, d), lambda i: (i, 0))
    w_spec = pl.BlockSpec((1, d), lambda i: (0, 0))
    if wpre is None:
        (h_new,), _ = _call(
            _post_last_body, grid=(m // rows,),
            in_specs=[row_spec, row_spec, w_spec], out_specs=[row_spec],
            out_shape=[jax.ShapeDtypeStruct((m, d), F32)],
            args=[h, m_, wpost.reshape(1, d)], name="post_last")
        return h_new, None
    (h_new, u), _ = _call(
        _post_body, grid=(m // rows,),
        in_specs=[row_spec, row_spec, w_spec, w_spec], out_specs=[row_spec, row_spec],
        out_shape=[jax.ShapeDtypeStruct((m, d), F32), jax.ShapeDtypeStruct((m, d), BF16)],
        args=[h, m_, wpost.reshape(1, d), wpre.reshape(1, d)], name="post")
    return h_new, u


_NT = (((1,), (1,)), ((), ()))


def _block_grid(m, n, bm, bn):
    assert m % bm == 0 and n % bn == 0, (m, n, bm, bn)
    return (m // bm, n // bn)


def _mm_body(a_ref, b_ref, o_ref):
    o_ref[...] = jnp.dot(a_ref[...], b_ref[...],
                         preferred_element_type=F32).astype(o_ref.dtype)


def _mm(a, b, out_dtype, bm, bn, name):
    m, k = a.shape
    n = b.shape[1]
    (o,), _ = _call(
        _mm_body,
        grid=_block_grid(m, n, bm, bn),
        in_specs=[pl.BlockSpec((bm, k), lambda i, j: (i, 0)),
                  pl.BlockSpec((k, bn), lambda i, j: (0, j))],
        out_specs=[pl.BlockSpec((bm, bn), lambda i, j: (i, j))],
        out_shape=[jax.ShapeDtypeStruct((m, n), out_dtype)],
        args=[a, b], name=name)
    return o


def _in_proj_body(a_ref, bt_ref, wfb_ref, o_ref, fb_ref):
    a = a_ref[...]
    o_ref[...] = lax.dot_general(a, bt_ref[...].astype(BF16), _NT,
                                 preferred_element_type=F32).astype(o_ref.dtype)

    @pl.when(pl.program_id(1) == 0)
    def _fb():
        wt = wfb_ref[...].astype(BF16)
        wt = jnp.concatenate(
            [wt, jnp.zeros((LANES - GATE_RANK, wt.shape[1]), BF16)], axis=0)
        fb_ref[...] = lax.dot_general(a, wt, _NT, preferred_element_type=F32)


def _in_proj(a, w_t, layer, bm, bn):
    m, k = a.shape
    (proj, fb), _ = _call(
        _in_proj_body,
        grid=_block_grid(m, MAIN_COLS, bm, bn),
        in_specs=[pl.BlockSpec((bm, k), lambda i, j: (i, 0)),
                  pl.BlockSpec((None, bn, k), lambda i, j: (layer, j, 0)),
                  pl.BlockSpec((None, GATE_RANK, k),
                               lambda i, j: (layer, MAIN_COLS // GATE_RANK, 0))],
        out_specs=[pl.BlockSpec((bm, bn), lambda i, j: (i, j)),
                   pl.BlockSpec((bm, LANES), lambda i, j: (i, 0))],
        out_shape=[jax.ShapeDtypeStruct((m, MAIN_COLS), BF16),
                   jax.ShapeDtypeStruct((m, LANES), F32)],
        args=[a, w_t, w_t], name="in_proj")
    return proj, fb


def _mm2_body(a1_ref, a2_ref, b1_ref, b2_ref, o_ref):
    acc = jnp.dot(a1_ref[...], b1_ref[...], preferred_element_type=F32)
    acc += jnp.dot(a2_ref[...], b2_ref[...], preferred_element_type=F32)
    o_ref[...] = acc.astype(o_ref.dtype)


def _mm2(a1, a2, b, out_dtype, bm, bn, name):
    m, k1 = a1.shape
    _, k2 = a2.shape
    assert k1 == k2
    n = b.shape[1]
    (o,), _ = _call(
        _mm2_body,
        grid=_block_grid(m, n, bm, bn),
        in_specs=[pl.BlockSpec((bm, k1), lambda i, j: (i, 0)),
                  pl.BlockSpec((bm, k2), lambda i, j: (i, 0)),
                  pl.BlockSpec((k1, bn), lambda i, j: (0, j)),
                  pl.BlockSpec((k2, bn), lambda i, j: (1, j))],
        out_specs=[pl.BlockSpec((bm, bn), lambda i, j: (i, j))],
        out_shape=[jax.ShapeDtypeStruct((m, n), out_dtype)],
        args=[a1, a2, b, b], name=name)
    return o


def _gateup_body(a_ref, bg_ref, bu_ref, o_ref):
    bg = bg_ref[...].astype(BF16)
    bu = bu_ref[...].astype(BF16)
    rows_per = a_ref.shape[0] // GATEUP_ROW_SPLIT
    for r in range(GATEUP_ROW_SPLIT):
        rows = slice(r * rows_per, (r + 1) * rows_per)
        a = a_ref[rows, :]
        g = jnp.dot(a, bg, preferred_element_type=F32)
        u = jnp.dot(a, bu, preferred_element_type=F32)
        o_ref[rows, :] = (g * jax.nn.sigmoid(g) * u).astype(o_ref.dtype)


def _gateup(a, wg, wu, layer, bm, bn, passengers=()):
    m, k = a.shape
    n = wg.shape[2]
    w_spec = pl.BlockSpec((None, k, bn), lambda i, j: (layer, 0, j))
    return _call(
        _gateup_body,
        grid=_block_grid(m, n, bm, bn),
        in_specs=[pl.BlockSpec((bm, k), lambda i, j: (i, 0)), w_spec, w_spec],
        out_specs=[pl.BlockSpec((bm, bn), lambda i, j: (i, j))],
        out_shape=[jax.ShapeDtypeStruct((m, n), BF16)],
        args=[a, wg, wu], name="ffn_gateup", passengers=passengers)


def _attn_body(brow_ref, q_ref, k0_ref, k1_ref, k2_ref, v0_ref, v1_ref, v2_ref,
               o_ref, bias_ref):
    qi = pl.program_id(2)

    @pl.when(qi <= ATT_NKB - 1)
    def _build_bias():
        qc = lax.broadcasted_iota(jnp.int32, (ATT_TQ, ATT_TK), 0) // CHUNK
        col = lax.broadcasted_iota(jnp.int32, (ATT_TQ, ATT_TK), 1)
        kc = col // CHUNK
        first_valid = (ATT_NKB - 1 - qi) * ATT_TQ
        keep = (kc >= qc) & (kc <= qc + LEFT_CHUNKS) & (col >= first_valid)
        for h in range(ATT_HB):
            rows = jnp.broadcast_to(brow_ref[h] * LOG2E, (ATT_TQ, BIAS_W))
            rolled = pltpu.roll(rows, 0, 1, stride=1, stride_axis=0)
            bias_ref[h] = jnp.where(keep, rolled[:, :ATT_TK], NEG_INF)

    scale = A_HEAD_DIM ** -0.5 * LOG2E
    def scores(h):
        lanes = slice(h * A_HEAD_DIM, (h + 1) * A_HEAD_DIM)
        q = (q_ref[0, :, lanes].astype(F32) * scale).astype(BF16)
        k = jnp.concatenate([k0_ref[0, :, lanes], k1_ref[0, :, lanes],
                             k2_ref[0, :, lanes]], axis=0)
        return lax.dot_general(q, k, _NT, preferred_element_type=F32) + bias_ref[h]

    lane = lax.broadcasted_iota(jnp.int32, (ATT_TK, A_HEAD_DIM), 1)
    ones_col = jnp.where(lane == 0, 1.0, 0.0).astype(BF16)

    s_next = scores(0)
    for h in range(ATT_HB):
        lanes = slice(h * A_HEAD_DIM, (h + 1) * A_HEAD_DIM)
        s = s_next
        if h + 1 < ATT_HB:
            s_next = scores(h + 1)
        v = jnp.concatenate([v0_ref[0, :, lanes], v1_ref[0, :, lanes],
                             v2_ref[0, :, lanes]], axis=0)
        s_top = s[:ATT_TQ // 2, :ATT_TK - ATT_SKIP]
        s_bot = s[ATT_TQ // 2:, ATT_SKIP:]
        p_top = jnp.exp2(s_top - jnp.max(s_top, axis=-1, keepdims=True)).astype(BF16)
        p_bot = jnp.exp2(s_bot - jnp.max(s_bot, axis=-1, keepdims=True)).astype(BF16)
        zeros = jnp.zeros((ATT_TQ // 2, ATT_SKIP), BF16)
        p = jnp.concatenate([jnp.concatenate([p_top, zeros], axis=1),
                             jnp.concatenate([zeros, p_bot], axis=1)], axis=0)
        ol = jnp.dot(p, jnp.concatenate([v, ones_col], axis=1),
                     preferred_element_type=F32)
        o = ol[:, :A_HEAD_DIM]
        l = ol[:, A_HEAD_DIM:A_HEAD_DIM + 1]
        o_ref[0, :, lanes] = (o / l).astype(o_ref.dtype)


def _band_attention(proj3, brow):
    b, s, _ = proj3.shape
    nq = s // ATT_TQ
    ng = A_HEADS // ATT_HB
    gq = COL_QA // ATT_W
    gk = COL_KA // ATT_W
    gv = COL_VA // ATT_W

    def kv_spec(col0, back):
        return pl.BlockSpec(
            (1, ATT_TQ, ATT_W),
            lambda bi, g, qi: (bi, jnp.maximum(qi - back, 0), col0 + g))

    (ya,), _ = _call(
        _attn_body,
        grid=(b, ng, nq),
        in_specs=[pl.BlockSpec((ATT_HB, 1, BIAS_W), lambda bi, g, qi: (g, 0, 0)),
                  pl.BlockSpec((1, ATT_TQ, ATT_W), lambda bi, g, qi: (bi, qi, gq + g)),
                  kv_spec(gk, 2), kv_spec(gk, 1), kv_spec(gk, 0),
                  kv_spec(gv, 2), kv_spec(gv, 1), kv_spec(gv, 0)],
        out_specs=[pl.BlockSpec((1, ATT_TQ, ATT_W), lambda bi, g, qi: (bi, qi, g))],
        out_shape=[jax.ShapeDtypeStruct((b, s, A_WIDTH), BF16)],
        scratch_shapes=[pltpu.VMEM((ATT_HB, ATT_TQ, ATT_TK), F32)],
        args=[brow] + [proj3] * 7, name="band_attention")
    return ya


def _bias_rows(rel_bias):
    far = rel_bias[:, 2 * REL_CLIP:]
    near = rel_bias[:, :1]
    n_far = LEFT_CHUNKS * CHUNK - REL_CLIP
    n_near = ATT_TK - n_far - (2 * REL_CLIP + 1)
    row = jnp.concatenate([
        jnp.broadcast_to(far, (A_HEADS, n_far)),
        jnp.flip(rel_bias, axis=1),
        jnp.broadcast_to(near, (A_HEADS, n_near)),
        jnp.broadcast_to(far, (A_HEADS, BIAS_W - ATT_TK)),
    ], axis=1)
    return row.reshape(A_HEADS, 1, BIAS_W)


def _split2(x):
    hi = x.astype(BF16)
    lo = (x - hi.astype(F32)).astype(BF16)
    return hi, lo


def _gla_body(q_ref, k_ref, v_ref, g_ref, fb_ref, wfg_ref, bfg_ref, nw_ref,
              o_ref, st_ref):
    step = pl.program_id(1)

    @pl.when(step == 0)
    def _reset():
        st_ref[...] = jnp.zeros_like(st_ref)

    z = jnp.dot(fb_ref[0].astype(BF16), wfg_ref[...],
                preferred_element_type=F32) + bfg_ref[...]
    log_a = (jnp.minimum(z, 0.0) - jnp.log(1.0 + jnp.exp(-jnp.abs(z)))) / GATE_TEMP
    row = lax.broadcasted_iota(jnp.int32, (GLA_ROWS, GLA_ROWS), 0)
    colm = lax.broadcasted_iota(jnp.int32, (GLA_ROWS, GLA_ROWS), 1)
    causal = (colm <= row) & (colm // CHUNK == row // CHUNK)
    tri = jnp.where(causal, 1.0, 0.0).astype(BF16)
    hi, lo = _split2(log_a)
    cum = (jnp.dot(tri, hi, preferred_element_type=F32)
           + jnp.dot(tri, lo, preferred_element_type=F32))

    q_all = q_ref[0].astype(F32) * (B_HEAD_K ** -0.5)
    k_all = k_ref[0].astype(F32)
    q_dec_all = (q_all * jnp.exp(cum)).astype(BF16)
    k_inv_all = (k_all * jnp.exp(-cum)).astype(BF16)
    k_tail, e_last = [], []
    for c in range(GLA_ROWS // CHUNK):
        rows = slice(c * CHUNK, (c + 1) * CHUNK)
        last = cum[(c + 1) * CHUNK - 1:(c + 1) * CHUNK, :]
        k_tail.append((k_all[rows] * jnp.exp(last - cum[rows])).astype(BF16))
        e_last.append(jnp.exp(last))

    n_chunks = GLA_ROWS // CHUNK
    kls = [slice(h * B_HEAD_K, (h + 1) * B_HEAD_K) for h in range(B_HEADS)]
    vls = [slice(h * B_HEAD_V, (h + 1) * B_HEAD_V) for h in range(B_HEADS)]
    o_intra, upd = [], []
    for h in range(B_HEADS):
        v = v_ref[0, :, vls[h]]
        att = lax.dot_general(q_dec_all[:, kls[h]], k_inv_all[:, kls[h]], _NT,
                              preferred_element_type=F32)
        att = jnp.where(causal, att, 0.0).astype(BF16)
        o_intra.append(jnp.dot(att, v, preferred_element_type=F32))
        upd.append([lax.dot_general(v[c * CHUNK:(c + 1) * CHUNK],
                                    k_tail[c][:, kls[h]], (((0,), (0,)), ((), ())),
                                    preferred_element_type=F32)
                    for c in range(n_chunks)])

    st = [st_ref[h] for h in range(B_HEADS)]
    o_parts = [[] for _ in range(B_HEADS)]
    for c in range(n_chunks):
        rows = slice(c * CHUNK, (c + 1) * CHUNK)
        for h in range(B_HEADS):
            o_parts[h].append(o_intra[h][rows] + lax.dot_general(
                q_dec_all[rows, kls[h]], st[h].astype(BF16), _NT,
                preferred_element_type=F32))
            st[h] = st[h] * e_last[c][:, kls[h]] + upd[h][c]

    for h in range(B_HEADS):
        st_ref[h] = st[h]
        o = _rms(jnp.concatenate(o_parts[h], axis=0)) * nw_ref[...]
        g = g_ref[0, :, vls[h]].astype(F32)
        o_ref[0, :, vls[h]] = (o * (g * jax.nn.sigmoid(g))).astype(o_ref.dtype)


def _gla(proj3, fb3, wfg, bfg, nw, passengers=()):
    b, s, _ = proj3.shape
    cq = COL_QB // B_KEY_WIDTH
    ck = COL_KB // B_KEY_WIDTH
    cv = COL_VB // B_WIDTH
    cg = COL_GB // B_WIDTH
    return _call(
        _gla_body,
        grid=(b, s // GLA_ROWS),
        in_specs=[
            pl.BlockSpec((1, GLA_ROWS, B_KEY_WIDTH), lambda bi, c: (bi, c, cq)),
            pl.BlockSpec((1, GLA_ROWS, B_KEY_WIDTH), lambda bi, c: (bi, c, ck)),
            pl.BlockSpec((1, GLA_ROWS, B_WIDTH), lambda bi, c: (bi, c, cv)),
            pl.BlockSpec((1, GLA_ROWS, B_WIDTH), lambda bi, c: (bi, c, cg)),
            pl.BlockSpec((1, GLA_ROWS, LANES), lambda bi, c: (bi, c, 0)),
            pl.BlockSpec((LANES, B_KEY_WIDTH), lambda bi, c: (0, 0)),
            pl.BlockSpec((1, B_KEY_WIDTH), lambda bi, c: (0, 0)),
            pl.BlockSpec((1, B_HEAD_V), lambda bi, c: (0, 0)),
        ],
        out_specs=[pl.BlockSpec((1, GLA_ROWS, B_WIDTH), lambda bi, c: (bi, c, 0))],
        out_shape=[jax.ShapeDtypeStruct((b, s, B_WIDTH), BF16)],
        scratch_shapes=[pltpu.VMEM((B_HEADS, B_HEAD_V, B_HEAD_K), F32)],
        args=[proj3, proj3, proj3, proj3, fb3, wfg, bfg, nw], name="gla",
        passengers=passengers)


def kernel(x, norm_mix_pre, norm_mix_post, norm_ffn_pre, norm_ffn_post, w_in, rel_bias,
           w_fgate_up, b_fgate, gla_norm, w_out, w_ffn_gate, w_ffn_up, w_ffn_down):
    b, s, d = x.shape
    depth = w_in.shape[0]
    m = b * s
    w_fg = jnp.pad(w_fgate_up, ((0, 0), (0, LANES - GATE_RANK), (0, 0))).astype(BF16)
    w_in_t = jnp.swapaxes(w_in, 1, 2)
    h = x.reshape(m, d)
    u = _prenorm(h, norm_mix_pre[0])
    for i in range(depth):
        proj, fb = _in_proj(u, w_in_t, i, *IN_PROJ_BLOCK)
        proj3 = proj.reshape(b, s, MAIN_COLS)
        ya = _band_attention(proj3, _bias_rows(rel_bias[i]))
        (yb,), (wout_b,) = _gla(proj3, fb.reshape(b, s, LANES), w_fg[i],
                                b_fgate[i].reshape(1, B_KEY_WIDTH),
                                gla_norm[i].reshape(1, B_HEAD_V),
                                passengers=[Passenger(w_out, i)])
        mix = _mm2(ya.reshape(m, A_WIDTH), yb.reshape(m, B_WIDTH), wout_b,
                   BF16, *OUT_PROJ_BLOCK, "out_proj")
        h, u = _post(h, mix, norm_mix_post[i], norm_ffn_pre[i])
        (hid,), (wdown_b,) = _gateup(u, w_ffn_gate, w_ffn_up, i, *GATEUP_BLOCK,
                                     passengers=[Passenger(w_ffn_down, i)])
        f = _mm(hid, wdown_b, BF16, *DOWN_BLOCK, "ffn_down")
        h, u = _post(h, f, norm_ffn_post[i], norm_mix_pre[i + 1] if i + 1 < depth else None)
    return h.reshape(b, s, d)
```

```python
from typing import NamedTuple

import jax
import jax.numpy as jnp
from jax import lax
from jax.experimental import pallas as pl
from jax.experimental.pallas import tpu as pltpu

CHUNK = 64
LEFT_CHUNKS = 8
A_WIDTH = 2048
A_HEAD_DIM = 128
A_HEADS = A_WIDTH // A_HEAD_DIM
REL_CLIP = 128
B_WIDTH = 2048
B_HEADS = 4
B_HEAD_V = B_WIDTH // B_HEADS
B_KEY_WIDTH = B_WIDTH // 2
B_HEAD_K = B_KEY_WIDTH // B_HEADS
GATE_RANK = 16
GATE_TEMP = 16.0
EPS = 1e-6
NEG_INF = -1e30
LOG2E = 1.4426950408889634

LANES = 128
BF16_SUBLANES = 16
VMEM_LIMIT = 60 * 1024 * 1024

IN_PROJ_BLOCK = (1024, 768)
OUT_PROJ_BLOCK = (1024, 1024)
GATEUP_BLOCK = (2048, 256)
DOWN_BLOCK = (512, 512)
STREAM_BUFFERS = 3
GATEUP_ROW_SPLIT = 2

COL_QA = 0
COL_KA = A_WIDTH
COL_VA = 2 * A_WIDTH
COL_QB = 3 * A_WIDTH
COL_KB = COL_QB + B_KEY_WIDTH
COL_VB = COL_KB + B_KEY_WIDTH
COL_GB = COL_VB + B_WIDTH
MAIN_COLS = COL_GB + B_WIDTH

ATT_TQ = 4 * CHUNK
ATT_NKB = LEFT_CHUNKS * CHUNK // ATT_TQ + 1
ATT_TK = ATT_NKB * ATT_TQ
ATT_HB = A_HEADS
ATT_W = ATT_HB * A_HEAD_DIM
BIAS_W = ATT_TK + ATT_TQ
ATT_SKIP = ATT_TQ // 2
GLA_ROWS = 4 * CHUNK

BF16 = jnp.bfloat16
F32 = jnp.float32


class Passenger(NamedTuple):
    src: jax.Array
    layer: int


def _slab_count(rows, n_steps):
    tiles = rows // BF16_SUBLANES
    assert tiles * BF16_SUBLANES == rows
    return max(s for s in range(1, min(tiles, n_steps) + 1) if tiles % s == 0)


def _call(body, *, grid, in_specs, out_specs, out_shape, args, name,
          scratch_shapes=(), passengers=()):
    n_in, n_out, n_pass = len(in_specs), len(out_specs), len(passengers)
    strides = [1] * len(grid)
    for ax in range(len(grid) - 2, -1, -1):
        strides[ax] = strides[ax + 1] * grid[ax + 1]
    n_steps = strides[0] * grid[0]

    in_specs, out_specs, out_shape, args = (list(in_specs), list(out_specs),
                                            list(out_shape), list(args))
    pass_in, pass_out, pass_shape = [], [], []
    for p in passengers:
        _, rows, cols = p.src.shape
        slabs = _slab_count(rows, n_steps)
        slab = rows // slabs

        def slab_of(*ids, last=slabs - 1):
            step = sum(i * s for i, s in zip(ids, strides))
            return jnp.minimum(step, last)

        pass_in.append(pl.BlockSpec(
            (None, slab, cols),
            lambda *ids, f=slab_of, layer=p.layer: (layer, f(*ids), 0)))
        pass_out.append(pl.BlockSpec((slab, cols), lambda *ids, f=slab_of: (f(*ids), 0)))
        pass_shape.append(jax.ShapeDtypeStruct((rows, cols), BF16))

    def kern(*refs):
        ins = refs[:n_in]
        srcs = refs[n_in:n_in + n_pass]
        outs = refs[n_in + n_pass:n_in + n_pass + n_out]
        dsts = refs[n_in + n_pass + n_out:n_in + 2 * n_pass + n_out]
        scratch = refs[n_in + 2 * n_pass + n_out:]
        for s_ref, d_ref in zip(srcs, dsts):
            d_ref[...] = s_ref[...].astype(BF16)
        body(*ins, *outs, *scratch)

    res = pl.pallas_call(
        kern,
        grid=grid,
        in_specs=in_specs + pass_in,
        out_specs=out_specs + pass_out,
        out_shape=out_shape + pass_shape,
        scratch_shapes=list(scratch_shapes),
        compiler_params=pltpu.CompilerParams(
            dimension_semantics=("arbitrary",) * len(grid),
            vmem_limit_bytes=VMEM_LIMIT),
        name=name,
    )(*args, *[p.src for p in passengers])
    return res[:n_out], res[n_out:]


def _rms(x):
    return x * lax.rsqrt(jnp.mean(x * x, axis=-1, keepdims=True) + EPS)


def _prenorm_body(x_ref, w_ref, u_ref):
    u_ref[...] = (_rms(x_ref[...]) * w_ref[...]).astype(u_ref.dtype)


def _prenorm(x, w, rows=512):
    m, d = x.shape
    (u,), _ = _call(
        _prenorm_body,
        grid=(m // rows,),
        in_specs=[pl.BlockSpec((rows, d), lambda i: (i, 0)),
                  pl.BlockSpec((1, d), lambda i: (0, 0))],
        out_specs=[pl.BlockSpec((rows, d), lambda i: (i, 0))],
        out_shape=[jax.ShapeDtypeStruct((m, d), BF16)],
        args=[x, w.reshape(1, d)],
        name="prenorm")
    return u


def _post_body(h_ref, m_ref, wpost_ref, wpre_ref, hout_ref, u_ref):
    h = h_ref[...] + _rms(m_ref[...].astype(F32)) * wpost_ref[...]
    hout_ref[...] = h
    u_ref[...] = (_rms(h) * wpre_ref[...]).astype(u_ref.dtype)


def _post_last_body(h_ref, m_ref, wpost_ref, hout_ref):
    hout_ref[...] = h_ref[...] + _rms(m_ref[...].astype(F32)) * wpost_ref[...]


def _post(h, m_, wpost, wpre, rows=256):
    m, d = h.shape
    row_spec = pl.BlockSpec((rows, d), lambda i: (i, 0))
    w_spec = pl.BlockSpec((1, d), lambda i: (0, 0))
    if wpre is None:
        (h_new,), _ = _call(
            _post_last_body, grid=(m // rows,),
            in_specs=[row_spec, row_spec, w_spec], out_specs=[row_spec],
            out_shape=[jax.ShapeDtypeStruct((m, d), F32)],
            args=[h, m_, wpost.reshape(1, d)], name="post_last")
        return h_new, None
    (h_new, u), _ = _call(
        _post_body, grid=(m // rows,),
        in_specs=[row_spec, row_spec, w_spec, w_spec], out_specs=[row_spec, row_spec],
        out_shape=[jax.ShapeDtypeStruct((m, d), F32), jax.ShapeDtypeStruct((m, d), BF16)],
        args=[h, m_, wpost.reshape(1, d), wpre.reshape(1, d)], name="post")
    return h_new, u


_NT = (((1,), (1,)), ((), ()))


def _block_grid(m, n, bm, bn):
    assert m % bm == 0 and n % bn == 0, (m, n, bm, bn)
    return (m // bm, n // bn)


def _mm_body(a_ref, b_ref, o_ref):
    o_ref[...] = jnp.dot(a_ref[...], b_ref[...],
                         preferred_element_type=F32).astype(o_ref.dtype)


def _mm(a, b, out_dtype, bm, bn, name):
    m, k = a.shape
    n = b.shape[1]
    grid = _block_grid(m, n, bm, bn)

    def outer(a_hbm, b_hbm, o_hbm):
        pltpu.emit_pipeline(
            _mm_body,
            grid=grid,
            in_specs=[pl.BlockSpec((bm, k), lambda i, j: (i, 0)),
                      pl.BlockSpec((k, bn), lambda i, j: (0, j),
                                   pipeline_mode=pl.Buffered(STREAM_BUFFERS))],
            out_specs=[pl.BlockSpec((bm, bn), lambda i, j: (i, j))],
        )(a_hbm, b_hbm, o_hbm)

    return pl.pallas_call(
        outer,
        in_specs=[pl.BlockSpec(memory_space=pl.ANY), pl.BlockSpec(memory_space=pl.ANY)],
        out_specs=pl.BlockSpec(memory_space=pl.ANY),
        out_shape=jax.ShapeDtypeStruct((m, n), out_dtype),
        compiler_params=pltpu.CompilerParams(vmem_limit_bytes=VMEM_LIMIT),
        name=name,
    )(a, b)


def _in_proj_body(a_ref, bt_ref, wfb_ref, o_ref, fb_ref):
    a = a_ref[...]
    o_ref[...] = lax.dot_general(a, bt_ref[...].astype(BF16), _NT,
                                 preferred_element_type=F32).astype(o_ref.dtype)

    @pl.when(pl.program_id(1) == 0)
    def _fb():
        wt = wfb_ref[...].astype(BF16)
        wt = jnp.concatenate(
            [wt, jnp.zeros((LANES - GATE_RANK, wt.shape[1]), BF16)], axis=0)
        fb_ref[...] = lax.dot_general(a, wt, _NT, preferred_element_type=F32)


def _in_proj(a, w_t, layer, bm, bn):
    m, k = a.shape
    (proj, fb), _ = _call(
        _in_proj_body,
        grid=_block_grid(m, MAIN_COLS, bm, bn),
        in_specs=[pl.BlockSpec((bm, k), lambda i, j: (i, 0)),
                  pl.BlockSpec((None, bn, k), lambda i, j: (layer, j, 0)),
                  pl.BlockSpec((None, GATE_RANK, k),
                               lambda i, j: (layer, MAIN_COLS // GATE_RANK, 0))],
        out_specs=[pl.BlockSpec((bm, bn), lambda i, j: (i, j)),
                   pl.BlockSpec((bm, LANES), lambda i, j: (i, 0))],
        out_shape=[jax.ShapeDtypeStruct((m, MAIN_COLS), BF16),
                   jax.ShapeDtypeStruct((m, LANES), F32)],
        args=[a, w_t, w_t], name="in_proj")
    return proj, fb


def _mm2_body(a1_ref, a2_ref, b1_ref, b2_ref, o_ref):
    acc = jnp.dot(a1_ref[...], b1_ref[...], preferred_element_type=F32)
    acc += jnp.dot(a2_ref[...], b2_ref[...], preferred_element_type=F32)
    o_ref[...] = acc.astype(o_ref.dtype)


def _mm2(a1, a2, b, out_dtype, bm, bn, name):
    m, k1 = a1.shape
    _, k2 = a2.shape
    assert k1 == k2
    n = b.shape[1]
    (o,), _ = _call(
        _mm2_body,
        grid=_block_grid(m, n, bm, bn),
        in_specs=[pl.BlockSpec((bm, k1), lambda i, j: (i, 0)),
                  pl.BlockSpec((bm, k2), lambda i, j: (i, 0)),
                  pl.BlockSpec((k1, bn), lambda i, j: (0, j)),
                  pl.BlockSpec((k2, bn), lambda i, j: (1, j))],
        out_specs=[pl.BlockSpec((bm, bn), lambda i, j: (i, j))],
        out_shape=[jax.ShapeDtypeStruct((m, n), out_dtype)],
        args=[a1, a2, b, b], name=name)
    return o


def _gateup_body(a_ref, bg_ref, bu_ref, o_ref):
    bg = bg_ref[...].astype(BF16)
    bu = bu_ref[...].astype(BF16)
    rows_per = a_ref.shape[0] // GATEUP_ROW_SPLIT
    for r in range(GATEUP_ROW_SPLIT):
        rows = slice(r * rows_per, (r + 1) * rows_per)
        a = a_ref[rows, :]
        g = jnp.dot(a, bg, preferred_element_type=F32)
        u = jnp.dot(a, bu, preferred_element_type=F32)
        o_ref[rows, :] = (g * jax.nn.sigmoid(g) * u).astype(o_ref.dtype)


def _gateup(a, wg, wu, layer, bm, bn, passengers=()):
    m, k = a.shape
    n = wg.shape[2]
    w_spec = pl.BlockSpec((None, k, bn), lambda i, j: (layer, 0, j))
    return _call(
        _gateup_body,
        grid=_block_grid(m, n, bm, bn),
        in_specs=[pl.BlockSpec((bm, k), lambda i, j: (i, 0)), w_spec, w_spec],
        out_specs=[pl.BlockSpec((bm, bn), lambda i, j: (i, j))],
        out_shape=[jax.ShapeDtypeStruct((m, n), BF16)],
        args=[a, wg, wu], name="ffn_gateup", passengers=passengers)


def _attn_body(brow_ref, q_ref, k0_ref, k1_ref, k2_ref, v0_ref, v1_ref, v2_ref,
               o_ref, bias_ref):
    qi = pl.program_id(2)

    @pl.when(qi <= ATT_NKB - 1)
    def _build_bias():
        qc = lax.broadcasted_iota(jnp.int32, (ATT_TQ, ATT_TK), 0) // CHUNK
        col = lax.broadcasted_iota(jnp.int32, (ATT_TQ, ATT_TK), 1)
        kc = col // CHUNK
        first_valid = (ATT_NKB - 1 - qi) * ATT_TQ
        keep = (kc >= qc) & (kc <= qc + LEFT_CHUNKS) & (col >= first_valid)
        for h in range(ATT_HB):
            rows = jnp.broadcast_to(brow_ref[h] * LOG2E, (ATT_TQ, BIAS_W))
            rolled = pltpu.roll(rows, 0, 1, stride=1, stride_axis=0)
            bias_ref[h] = jnp.where(keep, rolled[:, :ATT_TK], NEG_INF)

    scale = A_HEAD_DIM ** -0.5 * LOG2E
    def scores(h):
        lanes = slice(h * A_HEAD_DIM, (h + 1) * A_HEAD_DIM)
        q = (q_ref[0, :, lanes].astype(F32) * scale).astype(BF16)
        k = jnp.concatenate([k0_ref[0, :, lanes], k1_ref[0, :, lanes],
                             k2_ref[0, :, lanes]], axis=0)
        return lax.dot_general(q, k, _NT, preferred_element_type=F32) + bias_ref[h]

    lane = lax.broadcasted_iota(jnp.int32, (ATT_TK, A_HEAD_DIM), 1)
    ones_col = jnp.where(lane == 0, 1.0, 0.0).astype(BF16)

    s_next = scores(0)
    for h in range(ATT_HB):
        lanes = slice(h * A_HEAD_DIM, (h + 1) * A_HEAD_DIM)
        s = s_next
        if h + 1 < ATT_HB:
            s_next = scores(h + 1)
        v = jnp.concatenate([v0_ref[0, :, lanes], v1_ref[0, :, lanes],
                             v2_ref[0, :, lanes]], axis=0)
        s_top = s[:ATT_TQ // 2, :ATT_TK - ATT_SKIP]
        s_bot = s[ATT_TQ // 2:, ATT_SKIP:]
        p_top = jnp.exp2(s_top - jnp.max(s_top, axis=-1, keepdims=True)).astype(BF16)
        p_bot = jnp.exp2(s_bot - jnp.max(s_bot, axis=-1, keepdims=True)).astype(BF16)
        zeros = jnp.zeros((ATT_TQ // 2, ATT_SKIP), BF16)
        p = jnp.concatenate([jnp.concatenate([p_top, zeros], axis=1),
                             jnp.concatenate([zeros, p_bot], axis=1)], axis=0)
        ol = jnp.dot(p, jnp.concatenate([v, ones_col], axis=1),
                     preferred_element_type=F32)
        o = ol[:, :A_HEAD_DIM]
        l = ol[:, A_HEAD_DIM:A_HEAD_DIM + 1]
        o_ref[0, :, lanes] = (o / l).astype(o_ref.dtype)


def _band_attention(proj3, brow):
    b, s, _ = proj3.shape
    nq = s // ATT_TQ
    ng = A_HEADS // ATT_HB
    gq = COL_QA // ATT_W
    gk = COL_KA // ATT_W
    gv = COL_VA // ATT_W

    def kv_spec(col0, back):
        return pl.BlockSpec(
            (1, ATT_TQ, ATT_W),
            lambda bi, g, qi: (bi, jnp.maximum(qi - back, 0), col0 + g))

    (ya,), _ = _call(
        _attn_body,
        grid=(b, ng, nq),
        in_specs=[pl.BlockSpec((ATT_HB, 1, BIAS_W), lambda bi, g, qi: (g, 0, 0)),
                  pl.BlockSpec((1, ATT_TQ, ATT_W), lambda bi, g, qi: (bi, qi, gq + g)),
                  kv_spec(gk, 2), kv_spec(gk, 1), kv_spec(gk, 0),
                  kv_spec(gv, 2), kv_spec(gv, 1), kv_spec(gv, 0)],
        out_specs=[pl.BlockSpec((1, ATT_TQ, ATT_W), lambda bi, g, qi: (bi, qi, g))],
        out_shape=[jax.ShapeDtypeStruct((b, s, A_WIDTH), BF16)],
        scratch_shapes=[pltpu.VMEM((ATT_HB, ATT_TQ, ATT_TK), F32)],
        args=[brow] + [proj3] * 7, name="band_attention")
    return ya


def _bias_rows(rel_bias):
    far = rel_bias[:, 2 * REL_CLIP:]
    near = rel_bias[:, :1]
    n_far = LEFT_CHUNKS * CHUNK - REL_CLIP
    n_near = ATT_TK - n_far - (2 * REL_CLIP + 1)
    row = jnp.concatenate([
        jnp.broadcast_to(far, (A_HEADS, n_far)),
        jnp.flip(rel_bias, axis=1),
        jnp.broadcast_to(near, (A_HEADS, n_near)),
        jnp.broadcast_to(far, (A_HEADS, BIAS_W - ATT_TK)),
    ], axis=1)
    return row.reshape(A_HEADS, 1, BIAS_W)


def _split2(x):
    hi = x.astype(BF16)
    lo = (x - hi.astype(F32)).astype(BF16)
    return hi, lo


def _gla_body(q_ref, k_ref, v_ref, g_ref, fb_ref, wfg_ref, bfg_ref, nw_ref,
              o_ref, st_ref):
    step = pl.program_id(1)

    @pl.when(step == 0)
    def _reset():
        st_ref[...] = jnp.zeros_like(st_ref)

    z = jnp.dot(fb_ref[0].astype(BF16), wfg_ref[...],
                preferred_element_type=F32) + bfg_ref[...]
    log_a = (jnp.minimum(z, 0.0) - jnp.log(1.0 + jnp.exp(-jnp.abs(z)))) / GATE_TEMP
    row = lax.broadcasted_iota(jnp.int32, (GLA_ROWS, GLA_ROWS), 0)
    colm = lax.broadcasted_iota(jnp.int32, (GLA_ROWS, GLA_ROWS), 1)
    causal = (colm <= row) & (colm // CHUNK == row // CHUNK)
    tri = jnp.where(causal, 1.0, 0.0).astype(BF16)
    hi, lo = _split2(log_a)
    cum = (jnp.dot(tri, hi, preferred_element_type=F32)
           + jnp.dot(tri, lo, preferred_element_type=F32))

    q_all = q_ref[0].astype(F32) * (B_HEAD_K ** -0.5)
    k_all = k_ref[0].astype(F32)
    q_dec_all = (q_all * jnp.exp(cum)).astype(BF16)
    k_inv_all = (k_all * jnp.exp(-cum)).astype(BF16)
    k_tail, e_last = [], []
    for c in range(GLA_ROWS // CHUNK):
        rows = slice(c * CHUNK, (c + 1) * CHUNK)
        last = cum[(c + 1) * CHUNK - 1:(c + 1) * CHUNK, :]
        k_tail.append((k_all[rows] * jnp.exp(last - cum[rows])).astype(BF16))
        e_last.append(jnp.exp(last))

    n_chunks = GLA_ROWS // CHUNK
    kls = [slice(h * B_HEAD_K, (h + 1) * B_HEAD_K) for h in range(B_HEADS)]
    vls = [slice(h * B_HEAD_V, (h + 1) * B_HEAD_V) for h in range(B_HEADS)]
    o_intra, upd = [], []
    for h in range(B_HEADS):
        v = v_ref[0, :, vls[h]]
        att = lax.dot_general(q_dec_all[:, kls[h]], k_inv_all[:, kls[h]], _NT,
                              preferred_element_type=F32)
        att = jnp.where(causal, att, 0.0).astype(BF16)
        o_intra.append(jnp.dot(att, v, preferred_element_type=F32))
        upd.append([lax.dot_general(v[c * CHUNK:(c + 1) * CHUNK],
                                    k_tail[c][:, kls[h]], (((0,), (0,)), ((), ())),
                                    preferred_element_type=F32)
                    for c in range(n_chunks)])

    st = [st_ref[h] for h in range(B_HEADS)]
    o_parts = [[] for _ in range(B_HEADS)]
    for c in range(n_chunks):
        rows = slice(c * CHUNK, (c + 1) * CHUNK)
        for h in range(B_HEADS):
            o_parts[h].append(o_intra[h][rows] + lax.dot_general(
                q_dec_all[rows, kls[h]], st[h].astype(BF16), _NT,
                preferred_element_type=F32))
            st[h] = st[h] * e_last[c][:, kls[h]] + upd[h][c]

    for h in range(B_HEADS):
        st_ref[h] = st[h]
        o = _rms(jnp.concatenate(o_parts[h], axis=0)) * nw_ref[...]
        g = g_ref[0, :, vls[h]].astype(F32)
        o_ref[0, :, vls[h]] = (o * (g * jax.nn.sigmoid(g))).astype(o_ref.dtype)


def _gla(proj3, fb3, wfg, bfg, nw, passengers=()):
    b, s, _ = proj3.shape
    cq = COL_QB // B_KEY_WIDTH
    ck = COL_KB // B_KEY_WIDTH
    cv = COL_VB // B_WIDTH
    cg = COL_GB // B_WIDTH
    return _call(
        _gla_body,
        grid=(b, s // GLA_ROWS),
        in_specs=[
            pl.BlockSpec((1, GLA_ROWS, B_KEY_WIDTH), lambda bi, c: (bi, c, cq)),
            pl.BlockSpec((1, GLA_ROWS, B_KEY_WIDTH), lambda bi, c: (bi, c, ck)),
            pl.BlockSpec((1, GLA_ROWS, B_WIDTH), lambda bi, c: (bi, c, cv)),
            pl.BlockSpec((1, GLA_ROWS, B_WIDTH), lambda bi, c: (bi, c, cg)),
            pl.BlockSpec((1, GLA_ROWS, LANES), lambda bi, c: (bi, c, 0)),
            pl.BlockSpec((LANES, B_KEY_WIDTH), lambda bi, c: (0, 0)),
            pl.BlockSpec((1, B_KEY_WIDTH), lambda bi, c: (0, 0)),
            pl.BlockSpec((1, B_HEAD_V), lambda bi, c: (0, 0)),
        ],
        out_specs=[pl.BlockSpec((1, GLA_ROWS, B_WIDTH), lambda bi, c: (bi, c, 0))],
        out_shape=[jax.ShapeDtypeStruct((b, s, B_WIDTH), BF16)],
        scratch_shapes=[pltpu.VMEM((B_HEADS, B_HEAD_V, B_HEAD_K), F32)],
        args=[proj3, proj3, proj3, proj3, fb3, wfg, bfg, nw], name="gla",
        passengers=passengers)


def kernel(x, norm_mix_pre, norm_mix_post, norm_ffn_pre, norm_ffn_post, w_in, rel_bias,
           w_fgate_up, b_fgate, gla_norm, w_out, w_ffn_gate, w_ffn_up, w_ffn_down):
    b, s, d = x.shape
    depth = w_in.shape[0]
    m = b * s
    w_fg = jnp.pad(w_fgate_up, ((0, 0), (0, LANES - GATE_RANK), (0, 0))).astype(BF16)
    w_in_t = jnp.swapaxes(w_in, 1, 2)
    h = x.reshape(m, d)
    u = _prenorm(h, norm_mix_pre[0])
    for i in range(depth):
        proj, fb = _in_proj(u, w_in_t, i, *IN_PROJ_BLOCK)
        proj3 = proj.reshape(b, s, MAIN_COLS)
        ya = _band_attention(proj3, _bias_rows(rel_bias[i]))
        (yb,), (wout_b,) = _gla(proj3, fb.reshape(b, s, LANES), w_fg[i],
                                b_fgate[i].reshape(1, B_KEY_WIDTH),
                                gla_norm[i].reshape(1, B_HEAD_V),
                                passengers=[Passenger(w_out, i)])
        mix = _mm2(ya.reshape(m, A_WIDTH), yb.reshape(m, B_WIDTH), wout_b,
                   BF16, *OUT_PROJ_BLOCK, "out_proj")
        h, u = _post(h, mix, norm_mix_post[i], norm_ffn_pre[i])
        (hid,), (wdown_b,) = _gateup(u, w_ffn_gate, w_ffn_up, i, *GATEUP_BLOCK,
                                     passengers=[Passenger(w_ffn_down, i)])
        f = _mm(hid, wdown_b, BF16, *DOWN_BLOCK, "ffn_down")
        h, u = _post(h, f, norm_ffn_post[i], norm_mix_pre[i + 1] if i + 1 < depth else None)
    return h.reshape(b, s, d)
```
